```python
import jax
import jax.numpy as jnp
from jax import lax
import numpy as np

D_MODEL = 2048
BATCH = 2
SEQ = 4096
DEPTH = 4
DEC_BATCH = 8
DEC_SEQ = 1
PAST_LEN = 16384
PAGE_SIZE = 128

CONV_W = 4
D_LRU = D_MODEL // 2
LRU_BLOCKS = 16
LRU_BW = D_LRU // LRU_BLOCKS
LRU_C = 8.0
H_SSM = 16
P_SSM = D_MODEL // 32
D_SSM = H_SSM * P_SSM
G_SSM = 2
N_SSM = 128
D_XBC = D_SSM + 2 * G_SSM * N_SSM
SSD_CHUNK = 128
H_ATT = 16
HD = D_MODEL // 32
D_ATT = H_ATT * HD
Q_BLOCK = 128
N_BRANCH = 3
D_BR = D_LRU
D_FF = 4 * D_MODEL
EPS = 1e-6
SPLITS = (D_LRU, D_LRU, D_SSM, D_XBC, H_SSM, D_ATT, D_ATT, D_ATT, H_ATT, N_BRANCH * D_MODEL)
N_IN = sum(SPLITS)

kernel_name = 'hybrid_rglru_ssd_fox_step'


def _split_points():
    pts, acc = [], 0
    for s in SPLITS[:-1]:
        acc += s
        pts.append(acc)
    return pts


def rmsnorm(x, g):
    xf = x.astype(jnp.float32)
    y = xf * lax.rsqrt(jnp.mean(xf * xf, axis=-1, keepdims=True) + EPS)
    return (y * g.astype(jnp.float32)).astype(x.dtype)


def causal_conv(u, buf, w, b):
    L = u.shape[1]
    ext = jnp.concatenate([buf.astype(u.dtype), u], axis=1)
    out = b + ext[:, 0:L] * w[0]
    for j in range(1, CONV_W):
        out = out + ext[:, j:j + L] * w[j]
    return out, ext[:, L:]


def _lin_combine(e1, e2):
    a1, b1 = e1
    a2, b2 = e2
    return a1 * a2, a2 * b1 + b2


def rglru(xc, h0, wa, ba, wx, bx, lam):
    b, L, _ = xc.shape
    xb = xc.reshape(b, L, LRU_BLOCKS, LRU_BW)
    r = jax.nn.sigmoid(jnp.einsum('blkc,kcd->blkd', xb, wa).reshape(b, L, D_LRU) + ba)
    i = jax.nn.sigmoid(jnp.einsum('blkc,kcd->blkd', xb, wx).reshape(b, L, D_LRU) + bx)
    log_a = -LRU_C * jax.nn.softplus(-lam.astype(jnp.float32)) * r.astype(jnp.float32)
    a = jnp.exp(log_a)
    u = jnp.sqrt(-jnp.expm1(2.0 * log_a)) * (i * xc).astype(jnp.float32)
    a_cum, h_zero = lax.associative_scan(_lin_combine, (a, u), axis=1)
    h = a_cum * h0.astype(jnp.float32)[:, None, :] + h_zero
    return h, h[:, -1]


def ssd_scan(x, dt, A, B, C, h0, chunk):
    b, L = x.shape[:2]
    nc = L // chunk
    HG = H_SSM // G_SSM
    f32 = jnp.float32
    x = x.reshape(b, nc, chunk, G_SSM, HG, P_SSM).astype(f32)
    dt = dt.reshape(b, nc, chunk, G_SSM, HG)
    B = B.reshape(b, nc, chunk, G_SSM, N_SSM).astype(f32)
    C = C.reshape(b, nc, chunk, G_SSM, N_SSM).astype(f32)
    cs = jnp.cumsum(dt * A.astype(f32).reshape(G_SSM, HG), axis=2)
    seg = cs[:, :, :, None] - cs[:, :, None, :]
    mask = jnp.tril(jnp.ones((chunk, chunk), dtype=bool))[:, :, None, None]
    Lm = jnp.exp(jnp.where(mask, seg, -jnp.inf))
    CB = jnp.einsum('bctgn,bcsgn->bctsg', C, B)
    W = CB[..., None] * Lm * dt[:, :, None]
    y_diag = jnp.einsum('bctsgh,bcsghp->bctghp', W, x)
    decay_s = jnp.exp(cs[:, :, -1:] - cs)
    states = jnp.einsum('bcsgn,bcsgh,bcsghp->bcghpn', B, decay_s * dt, x)
    chunk_decay = jnp.exp(cs[:, :, -1])

    def step(h, inp):
        dec, st = inp
        return dec[..., None, None] * h + st, h

    h0g = h0.astype(f32).reshape(b, G_SSM, HG, P_SSM, N_SSM)
    h_last, h_prev = lax.scan(step, h0g, (jnp.moveaxis(chunk_decay, 1, 0), jnp.moveaxis(states, 1, 0)))
    h_prev = jnp.moveaxis(h_prev, 0, 1)
    y_off = jnp.einsum('bctgn,bcghpn,bctgh->bctghp', C, h_prev, jnp.exp(cs))
    y = (y_diag + y_off).reshape(b, L, H_SSM, P_SSM)
    return y, h_last.reshape(b, H_SSM, P_SSM, N_SSM)


def fox_attention(q, k, v, logf):
    b, Tq = q.shape[:2]
    T = k.shape[1]
    q_offset = T - Tq
    lf = logf.astype(jnp.float32)
    R = lax.cumsum(lf, axis=1, reverse=True) - lf
    Rk = jnp.moveaxis(R, 2, 1)
    qb = Q_BLOCK if Tq % Q_BLOCK == 0 else Tq
    kpos = jnp.arange(T)
    scale = HD ** -0.5

    def block(i):
        start = i * qb
        q_blk = lax.dynamic_slice_in_dim(q, start, qb, axis=1)
        Rq = lax.dynamic_slice_in_dim(Rk, q_offset + start, qb, axis=2)
        s = jnp.einsum('bqhd,bkhd->bhqk', q_blk, k).astype(jnp.float32) * scale
        s = s + Rk[:, :, None, :] - Rq[:, :, :, None]
        qpos = q_offset + start + jnp.arange(qb)
        s = jnp.where(kpos[None, :] <= qpos[:, None], s, -jnp.inf)
        p = jax.nn.softmax(s, axis=-1)
        return jnp.einsum('bhqk,bkhd->bqhd', p.astype(v.dtype), v)

    out = lax.map(block, jnp.arange(Tq // qb))
    return jnp.moveaxis(out, 0, 1).reshape(b, Tq, D_ATT)


def mixer_block(xn, p, lru_h0, lru_buf, ssm_h0, ssm_buf, k_past, v_past, logf_past):
    b, L, _ = xn.shape
    proj = jnp.einsum('bld,de->ble', xn, p['w_in'])
    u_lru, g_lru, z, xbc, dt_raw, q, k, v, f_raw, gates = jnp.split(proj, _split_points(), axis=-1)
    uc, lru_buf_new = causal_conv(u_lru, lru_buf, p['lru_conv_w'], p['lru_conv_b'])
    h, lru_h_new = rglru(uc, lru_h0, p['lru_wa'], p['lru_ba'], p['lru_wx'], p['lru_bx'], p['lru_lambda'])
    y_lru = h.astype(xn.dtype) * jax.nn.gelu(g_lru)
    xbc_c, ssm_buf_new = causal_conv(xbc, ssm_buf, p['ssm_conv_w'], p['ssm_conv_b'])
    xbc_c = jax.nn.silu(xbc_c)
    xs, Bm, Cm = jnp.split(xbc_c, [D_SSM, D_SSM + G_SSM * N_SSM], axis=-1)
    xs = xs.reshape(b, L, H_SSM, P_SSM)
    Bm = Bm.reshape(b, L, G_SSM, N_SSM)
    Cm = Cm.reshape(b, L, G_SSM, N_SSM)
    dt = jax.nn.softplus(dt_raw.astype(jnp.float32) + p['ssm_dt_bias'].astype(jnp.float32))
    A = -jnp.exp(p['ssm_a_log'].astype(jnp.float32))
    chunk = SSD_CHUNK if L % SSD_CHUNK == 0 else L
    ys, ssm_h_new = ssd_scan(xs, dt, A, Bm, Cm, ssm_h0, chunk)
    ys = ys + p['ssm_d'].astype(jnp.float32)[:, None] * xs.astype(jnp.float32)
    y_ssm = rmsnorm(ys.reshape(b, L, D_SSM).astype(xn.dtype) * jax.nn.silu(z), p['ssm_norm_g'])
    qh = rmsnorm(q.reshape(b, L, H_ATT, HD), p['att_q_norm_g'])
    kh = rmsnorm(k.reshape(b, L, H_ATT, HD), p['att_k_norm_g'])
    vh = v.reshape(b, L, H_ATT, HD)
    logf = jax.nn.log_sigmoid(f_raw.astype(jnp.float32) + p['att_f_bias'].astype(jnp.float32))
    if k_past is None:
        k_all, v_all, lf_all = kh, vh, logf
    else:
        k_all = jnp.concatenate([k_past.astype(kh.dtype), kh], axis=1)
        v_all = jnp.concatenate([v_past.astype(vh.dtype), vh], axis=1)
        lf_all = jnp.concatenate([logf_past.astype(jnp.float32), logf], axis=1)
    y_att = fox_attention(qh, k_all, v_all, lf_all)
    g = jax.nn.sigmoid(gates.reshape(b, L, N_BRANCH, D_MODEL) + p['b_gate'])
    branches = jnp.stack([y_lru, y_ssm, y_att], axis=2)
    proj_br = jnp.einsum('blnc,ncd->blnd', branches, p['w_branch'])
    merged = jnp.sum(g * proj_br, axis=2)
    out = jnp.einsum('bld,de->ble', merged, p['w_out'])
    return out, (kh, vh, logf, lru_h_new, lru_buf_new, ssm_h_new, ssm_buf_new)


def trunk_layer(x, p, lru_h0, lru_buf, ssm_h0, ssm_buf, k_past, v_past, logf_past):
    y, st = mixer_block(rmsnorm(x, p['norm1_g']), p, lru_h0, lru_buf, ssm_h0, ssm_buf, k_past, v_past, logf_past)
    x = x + y
    hdn = jax.nn.relu(jnp.einsum('bld,df->blf', rmsnorm(x, p['norm2_g']), p['w_up']))
    x = x + jnp.einsum('blf,fd->bld', hdn * hdn, p['w_down'])
    return x, st


def setup_inputs(seed: int = 0) -> dict:
    key = jax.random.key(seed)
    keys = jax.random.split(key, 48)
    ctr = [0]

    def nk():
        ctr[0] += 1
        return keys[ctr[0] - 1]

    def nrm(shape, scale=1.0):
        return jax.random.normal(nk(), shape, jnp.float32) * scale

    def unif(shape, lo, hi):
        return jax.random.uniform(nk(), shape, jnp.float32, minval=lo, maxval=hi)

    n_pages = PAST_LEN // PAGE_SIZE
    n_used = DEC_BATCH * n_pages
    n_pool = n_used + max(1, n_used // 4)
    page_table = jax.random.permutation(nk(), n_pool)[:n_used].reshape(DEC_BATCH, n_pages).astype(jnp.int32)

    s_lam = unif((DEPTH, D_LRU), 0.9, 0.999) ** (1.0 / LRU_C)
    lru_lambda = jnp.log(s_lam) - jnp.log1p(-s_lam)
    dt0 = jnp.exp(unif((DEPTH, H_SSM), float(np.log(1e-3)), float(np.log(1e-1))))
    ssm_dt_bias = dt0 + jnp.log(-jnp.expm1(-dt0))

    return {
        'x_prompt': nrm((BATCH, SEQ, D_MODEL)),
        'x_sample': nrm((DEC_BATCH, DEC_SEQ, D_MODEL)),
        'cache_k': nrm((DEPTH, n_pool, PAGE_SIZE, H_ATT, HD)),
        'cache_v': nrm((DEPTH, n_pool, PAGE_SIZE, H_ATT, HD)),
        'cache_logf': jax.nn.log_sigmoid(4.0 + nrm((DEPTH, n_pool, PAGE_SIZE, H_ATT), 0.5)),
        'state_lru_h': nrm((DEPTH, DEC_BATCH, D_LRU), 0.5),
        'state_lru_conv': nrm((DEPTH, DEC_BATCH, CONV_W - 1, D_LRU)),
        'state_ssm': nrm((DEPTH, DEC_BATCH, H_SSM, P_SSM, N_SSM), 0.1),
        'state_ssm_conv': nrm((DEPTH, DEC_BATCH, CONV_W - 1, D_XBC)),
        'page_table': page_table,
        'norm1_g': 1.0 + nrm((DEPTH, D_MODEL), 0.02),
        'w_in': nrm((DEPTH, D_MODEL, N_IN), D_MODEL ** -0.5),
        'b_gate': nrm((DEPTH, N_BRANCH, D_MODEL), 0.02),
        'lru_conv_w': nrm((DEPTH, CONV_W, D_LRU), CONV_W ** -0.5),
        'lru_conv_b': nrm((DEPTH, D_LRU), 0.02),
        'lru_wa': nrm((DEPTH, LRU_BLOCKS, LRU_BW, LRU_BW), LRU_BW ** -0.5),
        'lru_ba': nrm((DEPTH, D_LRU), 0.02),
        'lru_wx': nrm((DEPTH, LRU_BLOCKS, LRU_BW, LRU_BW), LRU_BW ** -0.5),
        'lru_bx': nrm((DEPTH, D_LRU), 0.02),
        'lru_lambda': lru_lambda,
        'ssm_conv_w': nrm((DEPTH, CONV_W, D_XBC), CONV_W ** -0.5),
        'ssm_conv_b': nrm((DEPTH, D_XBC), 0.02),
        'ssm_dt_bias': ssm_dt_bias,
        'ssm_a_log': jnp.log(unif((DEPTH, H_SSM), 1.0, 16.0)),
        'ssm_d': 1.0 + nrm((DEPTH, H_SSM), 0.1),
        'ssm_norm_g': 1.0 + nrm((DEPTH, D_SSM), 0.02),
        'att_q_norm_g': 1.0 + nrm((DEPTH, HD), 0.02),
        'att_k_norm_g': 1.0 + nrm((DEPTH, HD), 0.02),
        'att_f_bias': 4.0 + nrm((DEPTH, H_ATT), 0.5),
        'w_branch': nrm((DEPTH, N_BRANCH, D_BR, D_MODEL), D_BR ** -0.5),
        'w_out': nrm((DEPTH, D_MODEL, D_MODEL), D_MODEL ** -0.5),
        'norm2_g': 1.0 + nrm((DEPTH, D_MODEL), 0.02),
        'w_up': nrm((DEPTH, D_MODEL, D_FF), D_MODEL ** -0.5),
        'w_down': nrm((DEPTH, D_FF, D_MODEL), D_FF ** -0.5),
    }


def reference(x_prompt, x_sample, cache_k, cache_v, cache_logf, state_lru_h, state_lru_conv, state_ssm, state_ssm_conv, page_table, norm1_g, w_in, b_gate, lru_conv_w, lru_conv_b, lru_wa, lru_ba, lru_wx, lru_bx, lru_lambda, ssm_conv_w, ssm_conv_b, ssm_dt_bias, ssm_a_log, ssm_d, ssm_norm_g, att_q_norm_g, att_k_norm_g, att_f_bias, w_branch, w_out, norm2_g, w_up, w_down):
    b_p, s_p = x_prompt.shape[:2]
    b_s = x_sample.shape[0]
    dtp = x_prompt.dtype
    xp, xs = x_prompt, x_sample
    kp_l, vp_l, fp_l, ks_l, vs_l, fs_l = [], [], [], [], [], []
    hp_l, hs_l, cp_l, cs_l, sp_l, ss_l, scp_l, scs_l = [], [], [], [], [], [], [], []
    for l in range(DEPTH):
        p = {
            'norm1_g': norm1_g[l], 'w_in': w_in[l], 'b_gate': b_gate[l],
            'lru_conv_w': lru_conv_w[l], 'lru_conv_b': lru_conv_b[l],
            'lru_wa': lru_wa[l], 'lru_ba': lru_ba[l], 'lru_wx': lru_wx[l], 'lru_bx': lru_bx[l],
            'lru_lambda': lru_lambda[l],
            'ssm_conv_w': ssm_conv_w[l], 'ssm_conv_b': ssm_conv_b[l], 'ssm_dt_bias': ssm_dt_bias[l],
            'ssm_a_log': ssm_a_log[l], 'ssm_d': ssm_d[l], 'ssm_norm_g': ssm_norm_g[l],
            'att_q_norm_g': att_q_norm_g[l], 'att_k_norm_g': att_k_norm_g[l], 'att_f_bias': att_f_bias[l],
            'w_branch': w_branch[l], 'w_out': w_out[l],
            'norm2_g': norm2_g[l], 'w_up': w_up[l], 'w_down': w_down[l],
        }
        xp, st_p = trunk_layer(
            xp, p,
            jnp.zeros((b_p, D_LRU), dtp), jnp.zeros((b_p, CONV_W - 1, D_LRU), dtp),
            jnp.zeros((b_p, H_SSM, P_SSM, N_SSM), dtp), jnp.zeros((b_p, CONV_W - 1, D_XBC), dtp),
            None, None, None)
        k_past = cache_k[l][page_table].reshape(b_s, -1, H_ATT, HD)
        v_past = cache_v[l][page_table].reshape(b_s, -1, H_ATT, HD)
        f_past = cache_logf[l][page_table].reshape(b_s, -1, H_ATT)
        xs, st_s = trunk_layer(xs, p, state_lru_h[l], state_lru_conv[l], state_ssm[l], state_ssm_conv[l],
                               k_past, v_past, f_past)
        kh, vh, lf, lh, lc, sh, sc = st_p
        kp_l.append(kh.reshape(b_p, s_p // PAGE_SIZE, PAGE_SIZE, H_ATT, HD))
        vp_l.append(vh.reshape(b_p, s_p // PAGE_SIZE, PAGE_SIZE, H_ATT, HD))
        fp_l.append(lf.reshape(b_p, s_p // PAGE_SIZE, PAGE_SIZE, H_ATT))
        hp_l.append(lh)
        cp_l.append(lc)
        sp_l.append(sh)
        scp_l.append(sc)
        kh, vh, lf, lh, lc, sh, sc = st_s
        ks_l.append(kh)
        vs_l.append(vh)
        fs_l.append(lf)
        hs_l.append(lh)
        cs_l.append(lc)
        ss_l.append(sh)
        scs_l.append(sc)
    return (xp, xs,
            jnp.stack(kp_l), jnp.stack(vp_l), jnp.stack(fp_l),
            jnp.stack(ks_l), jnp.stack(vs_l), jnp.stack(fs_l),
            jnp.stack(hp_l), jnp.stack(hs_l),
            jnp.stack(cp_l), jnp.stack(cs_l),
            jnp.stack(sp_l), jnp.stack(ss_l),
            jnp.stack(scp_l), jnp.stack(scs_l))
```

```python
import functools

import jax
import jax.numpy as jnp
from jax import lax
from jax.experimental import pallas as pl
from jax.experimental.pallas import tpu as pltpu

F32 = jnp.float32
BF16 = jnp.bfloat16

D_MODEL = 2048
DEPTH = 4
PAGE = 128
CONV_W = 4
D_LRU = 1024
LRU_BLOCKS = 16
LRU_BW = 64
LRU_C = 8.0
H_SSM = 16
P_SSM = 64
D_SSM = 1024
G_SSM = 2
N_SSM = 128
D_XBC = D_SSM + 2 * G_SSM * N_SSM
H_ATT = 16
HD = 64
D_ATT = 1024
N_BRANCH = 3
D_FF = 4 * D_MODEL
EPS = 1e-6
NEG = -1e30

OFF_LRU = 0
OFF_Z = 2048
OFF_XBC = 3072
OFF_DT = 4608
OFF_QKV = 4624
OFF_F = 7696
OFF_GATES = 7712

VMEM_LIMIT = 56 * 1024 * 1024


def _cparams(sem):
    return pltpu.CompilerParams(dimension_semantics=sem, vmem_limit_bytes=VMEM_LIMIT)


def _dot(a, b):
    return jnp.dot(a, b, preferred_element_type=F32)


def _dot_nt(a, b):
    return lax.dot_general(a, b, (((1,), (1,)), ((), ())), preferred_element_type=F32)


def _split3(x):
    hi = x.astype(BF16)
    r = x - hi.astype(F32)
    mid = r.astype(BF16)
    lo = (r - mid.astype(F32)).astype(BF16)
    return hi, mid, lo


def _dot3_right(x, m01):
    hi, mid, lo = _split3(x)
    return _dot(hi, m01) + _dot(mid, m01) + _dot(lo, m01)


def _dot3_left(m01, x):
    hi, mid, lo = _split3(x)
    return _dot(m01, hi) + _dot(m01, mid) + _dot(m01, lo)


def _sigmoid(x):
    return 1.0 / (1.0 + jnp.exp(-x))


def _softplus(x):
    return jnp.maximum(x, 0.0) + jnp.log1p(jnp.exp(-jnp.abs(x)))


def _rms(x, g):
    ms = jnp.mean(x * x, axis=-1, keepdims=True)
    return x * lax.rsqrt(ms + EPS) * g


def _rms_kernel(x_ref, g_ref, o_ref):
    o_ref[...] = _rms(x_ref[...], g_ref[...]).astype(o_ref.dtype)


def _rmsnorm(x, g, bm):
    m, d = x.shape
    return pl.pallas_call(
        _rms_kernel,
        out_shape=jax.ShapeDtypeStruct((m, d), BF16),
        grid=(m // bm,),
        in_specs=[pl.BlockSpec((bm, d), lambda i: (i, 0)),
                  pl.BlockSpec((1, d), lambda i: (0, 0))],
        out_specs=pl.BlockSpec((bm, d), lambda i: (i, 0)),
        compiler_params=_cparams(("arbitrary",)),
        name="rmsnorm",
    )(x, g)


def _mm_nt_kernel(x_ref, w_ref, o_ref, wb_ref):
    @pl.when(pl.program_id(1) == 0)
    def _():
        wb_ref[...] = w_ref[...].astype(BF16)

    o_ref[...] = _dot_nt(x_ref[...], wb_ref[...]).astype(o_ref.dtype)


def _mm_nt(x, w_t, layer, off, n, bn, bm, name):
    m, k = x.shape
    return pl.pallas_call(
        _mm_nt_kernel,
        out_shape=jax.ShapeDtypeStruct((m, n), F32),
        grid=(n // bn, m // bm),
        in_specs=[pl.BlockSpec((bm, k), lambda j, i: (i, 0)),
                  pl.BlockSpec((None, pl.Element(bn), pl.Element(k)),
                               lambda j, i: (layer, pl.multiple_of(off + j * bn, 8), 0))],
        out_specs=pl.BlockSpec((bm, bn), lambda j, i: (i, j)),
        scratch_shapes=[pltpu.VMEM((bn, k), BF16)],
        compiler_params=_cparams(("arbitrary", "arbitrary")),
        name=name,
    )(x, w_t)


def _mm_small_kernel(x_ref, wa_ref, wb_ref, o_ref):
    x = x_ref[...]
    o_ref[:, 0:128] = _dot_nt(x, wa_ref[...].astype(BF16))
    o_ref[:, 128:256] = _dot_nt(x, wb_ref[...].astype(BF16))


def _mm_small(x, w_t, layer, bm):
    m, k = x.shape
    return pl.pallas_call(
        _mm_small_kernel,
        out_shape=jax.ShapeDtypeStruct((m, 256), F32),
        grid=(m // bm,),
        in_specs=[pl.BlockSpec((bm, k), lambda i: (i, 0)),
                  pl.BlockSpec((None, pl.Element(128), pl.Element(k)), lambda i: (layer, OFF_DT, 0)),
                  pl.BlockSpec((None, pl.Element(128), pl.Element(k)), lambda i: (layer, OFF_F, 0))],
        out_specs=pl.BlockSpec((bm, 256), lambda i: (i, 0)),
        compiler_params=_cparams(("arbitrary",)),
        name="inproj_small",
    )(x, w_t, w_t)


def _causal_conv(ext_ref, x, cw, cb, rows):
    ext_ref[8:8 + rows, :] = x
    out = cb + ext_ref[5:5 + rows, :] * cw[0:1]
    out = out + ext_ref[6:6 + rows, :] * cw[1:2]
    out = out + ext_ref[7:7 + rows, :] * cw[2:3]
    out = out + x * cw[3:4]
    ext_ref[0:8, :] = ext_ref[rows:rows + 8, :]
    return out


def _lru_kernel(u_ref, g_ref, cw_ref, cb_ref, wa_ref, wx_ref, ba_ref, bx_ref, lam_ref,
                h0_ref, tail0_ref, y_ref, hl_ref, ext_ref, a_ref, b_ref, p_ref, hc_ref,
                *, tt, last_row):
    t = pl.program_id(1)
    nt = pl.num_programs(1)

    @pl.when(t == 0)
    def _():
        ext_ref[0:8, :] = tail0_ref[...]
        hc_ref[...] = h0_ref[...]

    uc = _causal_conv(ext_ref, u_ref[...], cw_ref[...], cb_ref[...], tt)
    ucb = uc.astype(BF16)
    ra = jnp.concatenate([_dot(ucb[:, 256 * c:256 * (c + 1)], wa_ref[c]) for c in range(4)], axis=1)
    rx = jnp.concatenate([_dot(ucb[:, 256 * c:256 * (c + 1)], wx_ref[c]) for c in range(4)], axis=1)
    r = _sigmoid(ra + ba_ref[...])
    gi = _sigmoid(rx + bx_ref[...])
    log_a = (-LRU_C * _softplus(-lam_ref[...])) * r
    a = jnp.exp(log_a)
    th = jnp.tanh(log_a)
    one_m_a2 = -2.0 * th / (1.0 - th)
    bvec = jnp.sqrt(one_m_a2) * (gi * uc)
    nlc = D_LRU // 128
    for c in range(nlc):
        a_ref[c] = a[:, 128 * c:128 * (c + 1)]
        b_ref[c] = bvec[:, 128 * c:128 * (c + 1)]

    sub = tt // 8

    def step(j, carry):
        h, p = carry
        hs, ps = [], []
        for c in range(nlc):
            aa = a_ref[c, pl.ds(j, 8, stride=sub), :]
            bb = b_ref[c, pl.ds(j, 8, stride=sub), :]
            hc = aa * h[c] + bb
            pc = aa * p[c]
            b_ref[c, pl.ds(j, 8, stride=sub), :] = hc
            p_ref[c, pl.ds(j, 8, stride=sub), :] = pc
            hs.append(hc)
            ps.append(pc)
        return jnp.stack(hs), jnp.stack(ps)

    init = (jnp.zeros((nlc, 8, 128), F32), jnp.ones((nlc, 8, 128), F32))
    if sub == 1:
        hend, pend = step(0, init)
    else:
        hend, pend = lax.fori_loop(0, sub, step, init)
    hin = [hc_ref[...]]
    for s_ in range(8):
        he = jnp.concatenate([hend[c][s_:s_ + 1] for c in range(nlc)], axis=1)
        pe = jnp.concatenate([pend[c][s_:s_ + 1] for c in range(nlc)], axis=1)
        hin.append(he + pe * hin[s_])
    hc_ref[...] = hin[8]
    if sub == 1:
        hin_full = jnp.concatenate(hin[:8], axis=0)
    else:
        hin_full = jnp.concatenate([jnp.broadcast_to(hin[s_], (sub, D_LRU)) for s_ in range(8)], axis=0)
    h = jnp.concatenate([b_ref[c] + p_ref[c] * hin_full[:, 128 * c:128 * (c + 1)] for c in range(nlc)],
                        axis=1)
    g = g_ref[...]
    gelu = 0.5 * g * (1.0 + jnp.tanh(0.7978845608028654 * (g + 0.044715 * (g * g * g))))
    y_ref[...] = (h * gelu).astype(y_ref.dtype)

    @pl.when(t == nt - 1)
    def _():
        hl_ref[...] = h[last_row:last_row + 1]


def _lru_branch(lru, p, h0, tail0, nb, seq, tt, last_row):
    nt = seq // tt
    kern = functools.partial(_lru_kernel, tt=tt, last_row=last_row)
    vec = lambda: pl.BlockSpec((1, D_LRU), lambda b, t: (0, 0))
    return pl.pallas_call(
        kern,
        out_shape=(jax.ShapeDtypeStruct((nb * seq, D_LRU), BF16),
                   jax.ShapeDtypeStruct((nb, 1, D_LRU), F32)),
        grid=(nb, nt),
        in_specs=[pl.BlockSpec((tt, D_LRU), lambda b, t: (b * nt + t, 0)),
                  pl.BlockSpec((tt, D_LRU), lambda b, t: (b * nt + t, 1)),
                  pl.BlockSpec((CONV_W, D_LRU), lambda b, t: (0, 0)),
                  vec(),
                  pl.BlockSpec((4, 256, 256), lambda b, t: (0, 0, 0)),
                  pl.BlockSpec((4, 256, 256), lambda b, t: (0, 0, 0)),
                  vec(), vec(), vec(),
                  pl.BlockSpec((None, 1, D_LRU), lambda b, t: (b, 0, 0)),
                  pl.BlockSpec((None, 8, D_LRU), lambda b, t: (b, 0, 0))],
        out_specs=(pl.BlockSpec((tt, D_LRU), lambda b, t: (b * nt + t, 0)),
                   pl.BlockSpec((None, 1, D_LRU), lambda b, t: (b, 0, 0))),
        scratch_shapes=[pltpu.VMEM((tt + 8, D_LRU), F32),
                        pltpu.VMEM((D_LRU // 128, tt, 128), F32),
                        pltpu.VMEM((D_LRU // 128, tt, 128), F32),
                        pltpu.VMEM((D_LRU // 128, tt, 128), F32),
                        pltpu.VMEM((1, D_LRU), F32)],
        compiler_params=_cparams(("arbitrary", "arbitrary")),
        name="rglru",
    )(lru, lru, p["lru_conv_w"], p["lru_conv_b"], p["wa_bd"], p["wx_bd"],
      p["lru_ba"], p["lru_bx"], p["lru_lambda"], h0, tail0)


def _ssd_kernel(xbc_ref, z_ref, dtf_ref, cw_ref, cb_ref, dtb_ref, alog_ref, dx_ref, ng_ref,
                tri_ref, e_ref, st0_ref, tail0_ref, y_ref, stf_ref, ext_ref, st_ref, *, n_valid):
    c = pl.program_id(1)
    nc = pl.num_programs(1)
    q = PAGE

    @pl.when(c == 0)
    def _():
        ext_ref[0:8, :] = tail0_ref[...]
        st_ref[...] = st0_ref[...].T

    conv = _causal_conv(ext_ref, xbc_ref[...], cw_ref[...], cb_ref[...], q)
    xc = conv * _sigmoid(conv)
    xs = xc[:, 0:D_SSM]
    bm = xc[:, D_SSM:D_SSM + 256]
    cm = xc[:, D_SSM + 256:D_SSM + 512]

    row = lax.broadcasted_iota(jnp.int32, (q, q), 0)
    col = lax.broadcasted_iota(jnp.int32, (q, q), 1)
    dt = _softplus(dtf_ref[:, 0:128] + dtb_ref[...])
    if n_valid < q:
        dt = jnp.where(row < n_valid, dt, 0.0)
    a_neg = -jnp.exp(alog_ref[...])
    dta = dt * a_neg
    cs = _dot3_left(tri_ref[...], dta)
    cs_t = cs.T
    dt_t = dt.T
    cs_last = cs[q - 1:q, :]
    e01 = e_ref[...]
    exp_cs_x = _dot3_right(jnp.exp(cs), e01)
    ws_x = _dot3_right(jnp.exp(cs_last - cs) * dt, e01)
    dec_x = _dot3_right(jnp.broadcast_to(jnp.exp(cs_last), (8, 128)), e01)[0:1]
    dx = dx_ref[...]
    lane_lo = col < 64
    tril = col <= row

    ys = []
    for g in range(G_SSM):
        bg = bm[:, 128 * g:128 * (g + 1)]
        cgb = cm[:, 128 * g:128 * (g + 1)].astype(BF16)
        gl = slice(512 * g, 512 * (g + 1))
        cb_mat = _dot_nt(cgb, bg.astype(BF16))
        st_g = st_ref[:, gl]
        yoff = _dot(cgb, st_g.astype(BF16)) * exp_cs_x[:, gl]
        xw = (xs[:, gl] * ws_x[:, gl]).astype(BF16)
        st_ref[:, gl] = st_g * dec_x[:, gl] + _dot(bg.T.astype(BF16), xw)
        for jj in range(4):
            ws = []
            for a in range(2):
                h = 8 * g + 2 * jj + a
                seg = cs[:, h:h + 1] - cs_t[h:h + 1, :]
                lm = jnp.exp(jnp.where(tril, seg, NEG))
                ws.append((cb_mat * lm * dt_t[h:h + 1, :]).astype(BF16))
            wcat = jnp.concatenate(ws, axis=1)
            ll = slice(512 * g + 128 * jj, 512 * g + 128 * (jj + 1))
            x2 = xs[:, ll]
            xst = jnp.concatenate([jnp.where(lane_lo, x2, 0.0), jnp.where(lane_lo, 0.0, x2)],
                                  axis=0).astype(BF16)
            ys.append(_dot(wcat, xst) + yoff[:, 128 * jj:128 * (jj + 1)] + dx[:, ll] * x2)
    y = jnp.concatenate(ys, axis=1)
    z = z_ref[...]
    y_ref[...] = _rms(y * (z * _sigmoid(z)), ng_ref[...]).astype(y_ref.dtype)

    @pl.when(c == nc - 1)
    def _():
        stf_ref[...] = st_ref[...].T


def _ssd_branch(xbc, z, dtf, p, st0, tail0, nb, seq, n_valid):
    nc = seq // PAGE
    kern = functools.partial(_ssd_kernel, n_valid=n_valid)
    const = lambda shape: pl.BlockSpec(shape, lambda b, c: (0,) * len(shape))
    return pl.pallas_call(
        kern,
        out_shape=(jax.ShapeDtypeStruct((nb * seq, D_SSM), BF16),
                   jax.ShapeDtypeStruct((nb, D_SSM, N_SSM), F32)),
        grid=(nb, nc),
        in_specs=[pl.BlockSpec((PAGE, D_XBC), lambda b, c: (b * nc + c, 0)),
                  pl.BlockSpec((PAGE, D_SSM), lambda b, c: (b * nc + c, 0)),
                  pl.BlockSpec((PAGE, 256), lambda b, c: (b * nc + c, 0)),
                  const((CONV_W, D_XBC)), const((1, D_XBC)),
                  const((1, 128)), const((1, 128)), const((1, D_SSM)), const((1, D_SSM)),
                  const((128, 128)), const((128, D_SSM)),
                  pl.BlockSpec((None, D_SSM, N_SSM), lambda b, c: (b, 0, 0)),
                  pl.BlockSpec((None, 8, D_XBC), lambda b, c: (b, 0, 0))],
        out_specs=(pl.BlockSpec((PAGE, D_SSM), lambda b, c: (b * nc + c, 0)),
                   pl.BlockSpec((None, D_SSM, N_SSM), lambda b, c: (b, 0, 0))),
        scratch_shapes=[pltpu.VMEM((PAGE + 8, D_XBC), F32),
                        pltpu.VMEM((N_SSM, D_SSM), F32)],
        compiler_params=_cparams(("arbitrary", "arbitrary")),
        name="ssd",
    )(xbc, z, dtf, p["ssm_conv_w"], p["ssm_conv_b"], p["dt_bias128"], p["a_log128"],
      p["d_x"], p["ssm_norm_g"], p["tri_incl"], p["expand"], st0, tail0)


def _head_rms(x, bd, g):
    x2 = x * x
    hi = x2.astype(BF16)
    lo = (x2 - hi.astype(F32)).astype(BF16)
    ss = jnp.concatenate(
        [_dot(hi[:, 256 * c:256 * (c + 1)], bd) + _dot(lo[:, 256 * c:256 * (c + 1)], bd)
         for c in range(4)], axis=1)
    return x * lax.rsqrt(ss * (1.0 / HD) + EPS) * g


def _log_sigmoid(x):
    return -_softplus(-x)


def _prep_kernel(qkv_ref, dtf_ref, gq_ref, gk_ref, fb_ref, bd_ref, triu_ref, ones_ref,
                 q_ref, kt_ref, vt_ref, ktb_ref, vtb_ref, lft_ref, ft_ref, f_ref, car_ref):
    @pl.when(pl.program_id(1) == 0)
    def _():
        car_ref[...] = jnp.zeros_like(car_ref)

    bd = bd_ref[...]
    q = qkv_ref[:, 0:D_ATT]
    k = qkv_ref[:, D_ATT:2 * D_ATT]
    v = qkv_ref[:, 2 * D_ATT:3 * D_ATT]
    q_ref[...] = (_head_rms(q, bd, gq_ref[...]) * (HD ** -0.5)).astype(q_ref.dtype)
    kt = _head_rms(k, bd, gk_ref[...]).T
    kt_ref[...] = kt
    ktb_ref[...] = kt.astype(BF16)
    vt = v.T
    vt_ref[...] = vt
    vtb_ref[...] = vt.astype(BF16)

    lf_t = _log_sigmoid(dtf_ref[:, 128:256] + fb_ref[...]).T
    lft_ref[...] = lf_t[0:H_ATT]
    f_t = _dot3_right(lf_t, triu_ref[...]) + car_ref[...]
    car_ref[...] = car_ref[...] + _dot3_right(lf_t, ones_ref[...])
    ft_ref[...] = f_t[0:H_ATT]
    f_ref[...] = f_t.T


def _prep(qkv, dtf, p, nb, seq):
    npg = seq // PAGE
    m = nb * seq
    const = lambda shape: pl.BlockSpec(shape, lambda b, c: (0,) * len(shape))
    page = lambda rows: pl.BlockSpec((None, None, rows, PAGE), lambda b, c: (b, c, 0, 0))
    return pl.pallas_call(
        _prep_kernel,
        out_shape=(jax.ShapeDtypeStruct((m, D_ATT), BF16),
                   jax.ShapeDtypeStruct((nb, npg, D_ATT, PAGE), F32),
                   jax.ShapeDtypeStruct((nb, npg, D_ATT, PAGE), F32),
                   jax.ShapeDtypeStruct((nb, npg, D_ATT, PAGE), BF16),
                   jax.ShapeDtypeStruct((nb, npg, D_ATT, PAGE), BF16),
                   jax.ShapeDtypeStruct((nb, npg, H_ATT, PAGE), F32),
                   jax.ShapeDtypeStruct((nb, npg, H_ATT, PAGE), F32),
                   jax.ShapeDtypeStruct((m, 128), F32)),
        grid=(nb, npg),
        in_specs=[pl.BlockSpec((PAGE, 3 * D_ATT), lambda b, c: (b * npg + c, 0)),
                  pl.BlockSpec((PAGE, 256), lambda b, c: (b * npg + c, 0)),
                  const((1, D_ATT)), const((1, D_ATT)), const((1, 128)),
                  const((256, 256)), const((128, 128)), const((128, 128))],
        out_specs=(pl.BlockSpec((PAGE, D_ATT), lambda b, c: (b * npg + c, 0)),
                   page(D_ATT), page(D_ATT), page(D_ATT), page(D_ATT),
                   page(H_ATT), page(H_ATT),
                   pl.BlockSpec((PAGE, 128), lambda b, c: (b * npg + c, 0))),
        scratch_shapes=[pltpu.VMEM((128, 128), F32)],
        compiler_params=_cparams(("arbitrary", "arbitrary")),
        name="attn_prep",
    )(qkv, dtf, p["gq_x"], p["gk_x"], p["f_bias128"], p["bd_ones"], p["tri_incl_t"], p["ones128"])


def _attn_kernel(q_ref, kt_ref, vt_ref, f_ref, ft_ref, o_ref, *, tq, pages):
    j = pl.program_id(1)
    i = pl.program_id(2)
    tk = pages * PAGE
    assert tq == tk
    lane = lax.broadcasted_iota(jnp.int32, (tq, 128), 1)
    lane_lo = lane < 64
    sub_lo = lax.broadcasted_iota(jnp.int32, (128, tk), 0) < 64
    q2 = q_ref[...]
    fblk = f_ref[...]
    zero = jnp.zeros_like(q2)
    qa = [jnp.where(lane_lo, q2, zero), jnp.where(lane_lo, zero, q2)]
    fq = [jnp.sum(jnp.where(lane == 2 * j + a, fblk, 0.0), axis=-1, keepdims=True) for a in range(2)]
    diag = (lax.broadcasted_iota(jnp.int32, (tq, tk), 1) <= lax.broadcasted_iota(jnp.int32, (tq, tk), 0))

    def tile(kk, carry, masked):
        m0, l0, m1, l1, acc = carry
        ms, ls = [m0, m1], [l0, l1]
        kt = jnp.concatenate([kt_ref[kk * pages + pg] for pg in range(pages)], axis=1)
        vt = jnp.concatenate([vt_ref[kk * pages + pg] for pg in range(pages)], axis=1)
        vzero = jnp.zeros_like(vt)
        for a in range(2):
            fk = jnp.concatenate([ft_ref[a, pl.ds(kk * pages + pg, 1), :] for pg in range(pages)], axis=1)
            s = _dot(qa[a], kt) + (fq[a] - fk)
            if masked:
                s = jnp.where(diag, s, NEG)
            m_new = jnp.maximum(ms[a], jnp.max(s, axis=-1, keepdims=True))
            alpha = jnp.exp(ms[a] - m_new)
            pr = jnp.exp(s - m_new)
            ls[a] = alpha * ls[a] + jnp.sum(pr, axis=-1, keepdims=True)
            ms[a] = m_new
            vta = jnp.where(sub_lo, vt, vzero) if a == 0 else jnp.where(sub_lo, vzero, vt)
            scale = jnp.where(lane_lo, alpha, 1.0) if a == 0 else jnp.where(lane_lo, 1.0, alpha)
            acc = acc * scale + _dot_nt(pr.astype(BF16), vta)
        return ms[0], ls[0], ms[1], ls[1], acc

    init = (jnp.full((tq, 1), NEG, F32), jnp.zeros((tq, 1), F32),
            jnp.full((tq, 1), NEG, F32), jnp.zeros((tq, 1), F32),
            jnp.zeros((tq, 128), F32))
    carry = lax.fori_loop(0, i, lambda kk, c: tile(kk, c, False), init)
    _, l0, _, l1, acc = tile(i, carry, True)
    o_ref[...] = (acc * jnp.where(lane_lo, 1.0 / l0, 1.0 / l1)).astype(o_ref.dtype)


def _attention(qn, ktb, vtb, f, ft_t, nb, seq, tq=256):
    nq = seq // tq
    npg = seq // PAGE
    kern = functools.partial(_attn_kernel, tq=tq, pages=tq // PAGE)
    return pl.pallas_call(
        kern,
        out_shape=jax.ShapeDtypeStruct((nb * seq, D_ATT), BF16),
        grid=(nb, H_ATT // 2, nq),
        in_specs=[pl.BlockSpec((tq, 128), lambda b, j, i: (b * nq + i, j)),
                  pl.BlockSpec((None, npg, 128, PAGE), lambda b, j, i: (b, 0, j, 0)),
                  pl.BlockSpec((None, npg, 128, PAGE), lambda b, j, i: (b, 0, j, 0)),
                  pl.BlockSpec((tq, 128), lambda b, j, i: (b * nq + i, 0)),
                  pl.BlockSpec((None, 2, npg, PAGE), lambda b, j, i: (b, j, 0, 0))],
        out_specs=pl.BlockSpec((tq, 128), lambda b, j, i: (b * nq + i, j)),
        compiler_params=_cparams(("arbitrary", "arbitrary", "arbitrary")),
        name="fox_attention",
    )(qn, ktb, vtb, f, ft_t)


def _merge_kernel(y0_ref, y1_ref, y2_ref, gates_ref, bg_ref, x_ref, wb_ref, wo_ref, g2_ref,
                  x1_ref, xn_ref):
    ys = (y0_ref, y1_ref, y2_ref)
    merged = None
    for n in range(N_BRANCH):
        pb = _dot(ys[n][...], wb_ref[n])
        gate = _sigmoid(gates_ref[:, D_MODEL * n:D_MODEL * (n + 1)] + bg_ref[n:n + 1, :])
        merged = gate * pb if merged is None else merged + gate * pb
    x1 = x_ref[...] + _dot(merged.astype(BF16), wo_ref[...])
    x1_ref[...] = x1
    xn_ref[...] = _rms(x1, g2_ref[...]).astype(xn_ref.dtype)


def _merge(y_lru, y_ssm, y_att, gates, x, p, bm):
    m = x.shape[0]
    row = lambda w: pl.BlockSpec((bm, w), lambda i: (i, 0))
    return pl.pallas_call(
        _merge_kernel,
        out_shape=(jax.ShapeDtypeStruct((m, D_MODEL), F32),
                   jax.ShapeDtypeStruct((m, D_MODEL), BF16)),
        grid=(m // bm,),
        in_specs=[row(D_LRU), row(D_SSM), row(D_ATT), row(N_BRANCH * D_MODEL),
                  pl.BlockSpec((N_BRANCH, D_MODEL), lambda i: (0, 0)),
                  row(D_MODEL),
                  pl.BlockSpec((N_BRANCH, D_LRU, D_MODEL), lambda i: (0, 0, 0),
                               pipeline_mode=pl.Buffered(1)),
                  pl.BlockSpec((D_MODEL, D_MODEL), lambda i: (0, 0), pipeline_mode=pl.Buffered(1)),
                  pl.BlockSpec((1, D_MODEL), lambda i: (0, 0))],
        out_specs=(row(D_MODEL), row(D_MODEL)),
        compiler_params=_cparams(("arbitrary",)),
        name="merge_out",
    )(y_lru, y_ssm, y_att, gates, p["b_gate"], x, p["w_branch_bf"], p["w_out_bf"], p["norm2_g"])


def _mlp_kernel(xn_ref, x1_ref, wu_ref, wd_ref, gn_ref, o_ref, on_ref):
    f = pl.program_id(1)
    nf = pl.num_programs(1)
    h = jnp.maximum(_dot(xn_ref[...], wu_ref[...]), 0.0)
    contrib = _dot((h * h).astype(BF16), wd_ref[...])

    @pl.when(f == 0)
    def _():
        o_ref[...] = x1_ref[...] + contrib

    @pl.when(f > 0)
    def _():
        o_ref[...] = o_ref[...] + contrib

    @pl.when(f == nf - 1)
    def _():
        on_ref[...] = _rms(o_ref[...], gn_ref[...]).astype(on_ref.dtype)


def _mlp(x1n, x1, p, g_next, bm, bf=512):
    m = x1.shape[0]
    return pl.pallas_call(
        _mlp_kernel,
        out_shape=(jax.ShapeDtypeStruct((m, D_MODEL), F32),
                   jax.ShapeDtypeStruct((m, D_MODEL), BF16)),
        grid=(m // bm, D_FF // bf),
        in_specs=[pl.BlockSpec((bm, D_MODEL), lambda i, f: (i, 0)),
                  pl.BlockSpec((bm, D_MODEL), lambda i, f: (i, 0)),
                  pl.BlockSpec((D_MODEL, bf), lambda i, f: (0, f)),
                  pl.BlockSpec((bf, D_MODEL), lambda i, f: (f, 0)),
                  pl.BlockSpec((1, D_MODEL), lambda i, f: (0, 0))],
        out_specs=(pl.BlockSpec((bm, D_MODEL), lambda i, f: (i, 0)),
                   pl.BlockSpec((bm, D_MODEL), lambda i, f: (i, 0))),
        compiler_params=_cparams(("arbitrary", "arbitrary")),
        name="mlp",
    )(x1n, x1, p["w_up_bf"], p["w_down_bf"], g_next)


def _prep_s_kernel(qkv_ref, dtf_ref, gq_ref, gk_ref, fb_ref, bd_ref, q_ref, k_ref, lf_ref):
    bd = bd_ref[...]
    q_ref[...] = _head_rms(qkv_ref[:, 0:D_ATT], bd, gq_ref[...]) * (HD ** -0.5)
    k_ref[...] = _head_rms(qkv_ref[:, D_ATT:2 * D_ATT], bd, gk_ref[...])
    lf_ref[...] = _log_sigmoid(dtf_ref[:, 128:256] + fb_ref[...])


def _prep_s(qkv, dtf, p):
    nb = qkv.shape[0]
    full = lambda a: pl.BlockSpec(a.shape, lambda: (0,) * a.ndim)
    args = (qkv, dtf, p["gq_x"], p["gk_x"], p["f_bias128"], p["bd_ones"])
    return pl.pallas_call(
        _prep_s_kernel,
        out_shape=(jax.ShapeDtypeStruct((nb, D_ATT), F32),
                   jax.ShapeDtypeStruct((nb, D_ATT), F32),
                   jax.ShapeDtypeStruct((nb, 128), F32)),
        in_specs=[full(a) for a in args],
        out_specs=(pl.BlockSpec((nb, D_ATT), lambda: (0, 0)),
                   pl.BlockSpec((nb, D_ATT), lambda: (0, 0)),
                   pl.BlockSpec((nb, 128), lambda: (0, 0))),
        name="attn_prep_sample",
    )(*args)


def _dec_scores_kernel(pt_ref, qc_ref, *refs, group):
    k_refs = refs[:group]
    lf_refs = refs[group:2 * group]
    s_ref, lfo_ref = refs[2 * group:]
    qc = qc_ref[...]
    for g in range(group):
        prod = k_refs[g][...] * qc
        s_ref[g] = jnp.sum(prod.reshape(H_ATT, HD, PAGE), axis=1)
        lfo_ref[g] = lf_refs[g][...]


def _dec_scores(page_table, qcol, cache_kt, cache_lft, layer, group):
    nb, npg = page_table.shape
    k_specs = [pl.BlockSpec((None, None, D_ATT, PAGE),
                            functools.partial(lambda b, c, pt, g: (layer, pt[b, c * group + g], 0, 0), g=g))
               for g in range(group)]
    lf_specs = [pl.BlockSpec((None, None, H_ATT, PAGE),
                             functools.partial(lambda b, c, pt, g: (layer, pt[b, c * group + g], 0, 0), g=g))
                for g in range(group)]
    grid_spec = pltpu.PrefetchScalarGridSpec(
        num_scalar_prefetch=1,
        grid=(nb, npg // group),
        in_specs=[pl.BlockSpec((None, D_ATT, PAGE), lambda b, c, pt: (b, 0, 0))] + k_specs + lf_specs,
        out_specs=(pl.BlockSpec((None, group, H_ATT, PAGE), lambda b, c, pt: (b, c, 0, 0)),
                   pl.BlockSpec((None, group, H_ATT, PAGE), lambda b, c, pt: (b, c, 0, 0))),
    )
    return pl.pallas_call(
        functools.partial(_dec_scores_kernel, group=group),
        out_shape=(jax.ShapeDtypeStruct((nb, npg, H_ATT, PAGE), F32),
                   jax.ShapeDtypeStruct((nb, npg, H_ATT, PAGE), F32)),
        grid_spec=grid_spec,
        compiler_params=_cparams(("arbitrary", "arbitrary")),
        name="decode_scores",
    )(page_table, qcol, *([cache_kt] * group), *([cache_lft] * group))


def _dec_softmax_kernel(s_ref, lf_ref, qc_ref, kn_ref, lfn_ref, triu_ref, ones_ref,
                        p_ref, pn_ref, r_ref):
    npg = s_ref.shape[0]
    lf = lf_ref[...].reshape(npg * H_ATT, PAGE)
    r_ref[...] = _dot3_right(lf, triu_ref[...]).reshape(npg, H_ATT, PAGE)
    p_ref[...] = _dot3_right(lf, ones_ref[...]).reshape(npg, H_ATT, PAGE)

    def body(t, carry):
        pg = npg - 1 - t
        r_ref[pg] = r_ref[pg] + carry
        return carry + p_ref[pg]

    lax.fori_loop(0, npg, body, lfn_ref[...])
    s = s_ref[...] + r_ref[...]
    s_new = jnp.sum((qc_ref[...] * kn_ref[...]).reshape(H_ATT, HD, PAGE), axis=1)
    m = jnp.max(jnp.max(s, axis=0), axis=-1, keepdims=True)
    m = jnp.maximum(m, s_new)
    e = jnp.exp(s - m[None])
    e_new = jnp.exp(s_new - m)
    denom = jnp.sum(jnp.sum(e, axis=0), axis=-1, keepdims=True) + e_new
    inv = 1.0 / denom
    p_ref[...] = e * inv[None]
    pn_ref[...] = e_new * inv


def _dec_softmax(s, lfg, qcol, kncol, lfncol, p):
    nb, npg = s.shape[:2]
    blk4 = pl.BlockSpec((None, npg, H_ATT, PAGE), lambda b: (b, 0, 0, 0))
    col = pl.BlockSpec((None, D_ATT, PAGE), lambda b: (b, 0, 0))
    hrow = pl.BlockSpec((None, H_ATT, PAGE), lambda b: (b, 0, 0))
    c128 = pl.BlockSpec((128, 128), lambda b: (0, 0))
    return pl.pallas_call(
        _dec_softmax_kernel,
        out_shape=(jax.ShapeDtypeStruct((nb, npg, H_ATT, PAGE), F32),
                   jax.ShapeDtypeStruct((nb, H_ATT, PAGE), F32)),
        grid=(nb,),
        in_specs=[blk4, blk4, col, col, hrow, c128, c128],
        out_specs=(blk4, hrow),
        scratch_shapes=[pltpu.VMEM((npg, H_ATT, PAGE), F32)],
        compiler_params=_cparams(("arbitrary",)),
        name="decode_softmax",
    )(s, lfg, qcol, kncol, lfncol, p["tri_strict_t"], p["ones128"])


def _dec_values_kernel(pt_ref, p_ref, pn_ref, vn_ref, *refs, group):
    v_refs = refs[:group]
    o_ref = refs[group]
    acc_ref = refs[group + 1]
    c = pl.program_id(1)

    @pl.when(c == 0)
    def _():
        acc_ref[...] = jnp.zeros_like(acc_ref)

    acc = acc_ref[...]
    for g in range(group):
        pb = jnp.broadcast_to(p_ref[g][:, None, :], (H_ATT, HD, PAGE)).reshape(D_ATT, PAGE)
        acc = acc + v_refs[g][...] * pb
    acc_ref[...] = acc

    @pl.when(c == pl.num_programs(1) - 1)
    def _():
        pnb = jnp.broadcast_to(pn_ref[...][:, None, :], (H_ATT, HD, PAGE)).reshape(D_ATT, PAGE)
        o_ref[...] = jnp.sum(acc, axis=-1, keepdims=True) + pnb * vn_ref[...]


def _dec_values(page_table, pmat, pnew, vncol, cache_vt, layer, group):
    nb, npg = page_table.shape
    v_specs = [pl.BlockSpec((None, None, D_ATT, PAGE),
                            functools.partial(lambda b, c, pt, g: (layer, pt[b, c * group + g], 0, 0), g=g))
               for g in range(group)]
    grid_spec = pltpu.PrefetchScalarGridSpec(
        num_scalar_prefetch=1,
        grid=(nb, npg // group),
        in_specs=[pl.BlockSpec((None, group, H_ATT, PAGE), lambda b, c, pt: (b, c, 0, 0)),
                  pl.BlockSpec((None, H_ATT, PAGE), lambda b, c, pt: (b, 0, 0)),
                  pl.BlockSpec((None, D_ATT, PAGE), lambda b, c, pt: (b, 0, 0))] + v_specs,
        out_specs=pl.BlockSpec((None, D_ATT, PAGE), lambda b, c, pt: (b, 0, 0)),
        scratch_shapes=[pltpu.VMEM((D_ATT, PAGE), F32)],
    )
    return pl.pallas_call(
        functools.partial(_dec_values_kernel, group=group),
        out_shape=jax.ShapeDtypeStruct((nb, D_ATT, PAGE), F32),
        grid_spec=grid_spec,
        compiler_params=_cparams(("arbitrary", "arbitrary")),
        name="decode_values",
    )(page_table, pmat, pnew, vncol, *([cache_vt] * group))


def _block_diag_256(w):
    w4 = w.reshape(4, 4, LRU_BW, LRU_BW)
    eye = jnp.eye(4, dtype=w.dtype)
    bd = jnp.einsum("cjab,jk->cjakb", w4, eye)
    return bd.reshape(4, 256, 256).astype(BF16)


def _pad_lanes(v, width=128):
    return jnp.zeros((1, width), F32).at[0, :v.shape[0]].set(v)


def _constants():
    r = jnp.arange(128)
    tri_incl = (r[None, :] <= r[:, None]).astype(BF16)
    return {
        "tri_incl": tri_incl,
        "tri_incl_t": tri_incl.T,
        "tri_strict_t": (r[:, None] > r[None, :]).astype(BF16),
        "ones128": jnp.ones((128, 128), BF16),
        "expand": (jnp.arange(D_SSM)[None, :] // P_SSM == r[:, None]).astype(BF16),
        "bd_ones": (jnp.arange(256)[:, None] // HD == jnp.arange(256)[None, :] // HD).astype(BF16),
    }


def _layer_params(l, a, consts):
    p = dict(consts)
    p["lru_conv_w"] = a["lru_conv_w"][l]
    p["lru_conv_b"] = a["lru_conv_b"][l][None]
    p["wa_bd"] = _block_diag_256(a["lru_wa"][l])
    p["wx_bd"] = _block_diag_256(a["lru_wx"][l])
    p["lru_ba"] = a["lru_ba"][l][None]
    p["lru_bx"] = a["lru_bx"][l][None]
    p["lru_lambda"] = a["lru_lambda"][l][None]
    p["ssm_conv_w"] = a["ssm_conv_w"][l]
    p["ssm_conv_b"] = a["ssm_conv_b"][l][None]
    p["dt_bias128"] = _pad_lanes(a["ssm_dt_bias"][l])
    p["a_log128"] = _pad_lanes(a["ssm_a_log"][l])
    p["d_x"] = jnp.repeat(a["ssm_d"][l], P_SSM)[None]
    p["ssm_norm_g"] = a["ssm_norm_g"][l][None]
    p["gq_x"] = jnp.tile(a["att_q_norm_g"][l], H_ATT)[None]
    p["gk_x"] = jnp.tile(a["att_k_norm_g"][l], H_ATT)[None]
    p["f_bias128"] = _pad_lanes(a["att_f_bias"][l])
    p["b_gate"] = a["b_gate"][l]
    p["w_branch_bf"] = a["w_branch"][l].astype(BF16)
    p["w_out_bf"] = a["w_out"][l].astype(BF16)
    p["norm2_g"] = a["norm2_g"][l][None]
    p["w_up_bf"] = a["w_up"][l].astype(BF16)
    p["w_down_bf"] = a["w_down"][l].astype(BF16)
    return p


def _in_proj(xn, w_t, layer, bm):
    lru = _mm_nt(xn, w_t, layer, OFF_LRU, 2048, 1024, bm, "inproj_lru")
    z = _mm_nt(xn, w_t, layer, OFF_Z, 1024, 1024, bm, "inproj_z")
    xbc = _mm_nt(xn, w_t, layer, OFF_XBC, 1536, 768, bm, "inproj_xbc")
    qkv = _mm_nt(xn, w_t, layer, OFF_QKV, 3072, 1024, bm, "inproj_qkv")
    gates = _mm_nt(xn, w_t, layer, OFF_GATES, 6144, 1024, bm, "inproj_gates")
    dtf = _mm_small(xn, w_t, layer, bm)
    return lru, z, xbc, qkv, gates, dtf


def _pad_rows(x, rows):
    nb, w = x.shape
    return jnp.zeros((nb, rows, w), x.dtype).at[:, 0].set(x).reshape(nb * rows, w)


def _tail8(buf):
    return jnp.pad(buf, ((0, 0), (5, 0), (0, 0)))


def kernel(x_prompt, x_sample, cache_k, cache_v, cache_logf, state_lru_h, state_lru_conv, state_ssm, state_ssm_conv, page_table, norm1_g, w_in, b_gate, lru_conv_w, lru_conv_b, lru_wa, lru_ba, lru_wx, lru_bx, lru_lambda, ssm_conv_w, ssm_conv_b, ssm_dt_bias, ssm_a_log, ssm_d, ssm_norm_g, att_q_norm_g, att_k_norm_g, att_f_bias, w_branch, w_out, norm2_g, w_up, w_down):
    a = dict(b_gate=b_gate, lru_conv_w=lru_conv_w, lru_conv_b=lru_conv_b, lru_wa=lru_wa, lru_ba=lru_ba,
             lru_wx=lru_wx, lru_bx=lru_bx, lru_lambda=lru_lambda, ssm_conv_w=ssm_conv_w,
             ssm_conv_b=ssm_conv_b, ssm_dt_bias=ssm_dt_bias, ssm_a_log=ssm_a_log, ssm_d=ssm_d,
             ssm_norm_g=ssm_norm_g, att_q_norm_g=att_q_norm_g, att_k_norm_g=att_k_norm_g,
             att_f_bias=att_f_bias, w_branch=w_branch, w_out=w_out, norm2_g=norm2_g, w_up=w_up,
             w_down=w_down)
    nbp, seq, _ = x_prompt.shape
    nbs = x_sample.shape[0]
    mp = nbp * seq
    npg = seq // PAGE
    n_pool = cache_k.shape[1]
    consts = _constants()

    w_t = jnp.swapaxes(w_in, 1, 2)
    cache_kt = jnp.transpose(cache_k, (0, 1, 3, 4, 2)).reshape(DEPTH, n_pool, D_ATT, PAGE)
    cache_vt = jnp.transpose(cache_v, (0, 1, 3, 4, 2)).reshape(DEPTH, n_pool, D_ATT, PAGE)
    cache_lft = jnp.transpose(cache_logf, (0, 1, 3, 2))

    xp = x_prompt.reshape(mp, D_MODEL)
    xs = x_sample.reshape(nbs, D_MODEL)
    xpn = _rmsnorm(xp, norm1_g[0][None], 512)
    xsn = _rmsnorm(xs, norm1_g[0][None], nbs)

    zeros_h = jnp.zeros((nbp, 1, D_LRU), F32)
    zeros_lt = jnp.zeros((nbp, 8, D_LRU), F32)
    zeros_st = jnp.zeros((nbp, D_SSM, N_SSM), F32)
    zeros_xt = jnp.zeros((nbp, 8, D_XBC), F32)

    outs = [[] for _ in range(14)]
    for l in range(DEPTH):
        p = _layer_params(l, a, consts)
        g_next = norm1_g[(l + 1) % DEPTH][None]

        lru, z, xbc, qkv, gates, dtf = _in_proj(xpn, w_t, l, 1024)
        y_lru, h_last = _lru_branch(lru, p, zeros_h, zeros_lt, nbp, seq, 512, 511)
        y_ssm, st_last = _ssd_branch(xbc, z, dtf, p, zeros_st, zeros_xt, nbp, seq, PAGE)
        qn, kt, vt, ktb, vtb, lft, ft, f = _prep(qkv, dtf, p, nbp, seq)
        y_att = _attention(qn, ktb, vtb, f, jnp.swapaxes(ft, 1, 2), nbp, seq)
        x1, x1n = _merge(y_lru, y_ssm, y_att, gates, xp, p, 256)
        xp, xpn = _mlp(x1n, x1, p, g_next, 512)
        outs[0].append(kt.reshape(nbp, npg, H_ATT, HD, PAGE))
        outs[1].append(vt.reshape(nbp, npg, H_ATT, HD, PAGE))
        outs[2].append(lft)
        outs[6].append(h_last.reshape(nbp, D_LRU))
        outs[8].append(lru.reshape(nbp, seq, 2 * D_LRU)[:, seq - 3:, :D_LRU])
        outs[10].append(st_last.reshape(nbp, H_SSM, P_SSM, N_SSM))
        outs[12].append(xbc.reshape(nbp, seq, D_XBC)[:, seq - 3:, :])

        lru_s, z_s, xbc_s, qkv_s, gates_s, dtf_s = _in_proj(xsn, w_t, l, nbs)
        y_lru_s, h_s = _lru_branch(_pad_rows(lru_s, 8), p, state_lru_h[l][:, None, :],
                                   _tail8(state_lru_conv[l]), nbs, 8, 8, 0)
        y_ssm_s, st_s = _ssd_branch(_pad_rows(xbc_s, PAGE), _pad_rows(z_s, PAGE), _pad_rows(dtf_s, PAGE), p,
                                    state_ssm[l].reshape(nbs, D_SSM, N_SSM), _tail8(state_ssm_conv[l]),
                                    nbs, PAGE, 1)
        qs, ks, lfs = _prep_s(qkv_s, dtf_s, p)
        vs = qkv_s[:, 2 * D_ATT:]
        bcast = lambda v: jnp.broadcast_to(v[:, :, None], (nbs, v.shape[1], PAGE))
        qcol = bcast(qs)
        s_mat, lfg = _dec_scores(page_table, qcol, cache_kt, cache_lft, l, 16)
        pmat, pnew = _dec_softmax(s_mat, lfg, qcol, bcast(ks), bcast(lfs[:, :H_ATT]), p)
        o_s = _dec_values(page_table, pmat, pnew, bcast(vs), cache_vt, l, 16)
        y_att_s = o_s[:, :, 0].astype(BF16)
        x1s, x1sn = _merge(y_lru_s[::8], y_ssm_s[::PAGE], y_att_s, gates_s, xs, p, nbs)
        xs, xsn = _mlp(x1sn, x1s, p, g_next, nbs)
        outs[3].append(ks.reshape(nbs, 1, H_ATT, HD))
        outs[4].append(vs.reshape(nbs, 1, H_ATT, HD))
        outs[5].append(lfs[:, :H_ATT].reshape(nbs, 1, H_ATT))
        outs[7].append(h_s.reshape(nbs, D_LRU))
        outs[9].append(jnp.concatenate([state_lru_conv[l][:, 1:], lru_s[:, None, :D_LRU]], axis=1))
        outs[11].append(st_s.reshape(nbs, H_SSM, P_SSM, N_SSM))
        outs[13].append(jnp.concatenate([state_ssm_conv[l][:, 1:], xbc_s[:, None, :]], axis=1))

    st = [jnp.stack(o) for o in outs]
    new_k_p = jnp.transpose(st[0], (0, 1, 2, 5, 3, 4))
    new_v_p = jnp.transpose(st[1], (0, 1, 2, 5, 3, 4))
    new_lf_p = jnp.transpose(st[2], (0, 1, 2, 4, 3))
    return (xp.reshape(nbp, seq, D_MODEL), xs.reshape(nbs, 1, D_MODEL),
            new_k_p, new_v_p, new_lf_p, st[3], st[4], st[5],
            st[6], st[7], st[8], st[9], st[10], st[11], st[12], st[13])
```

```python
import functools

import jax
import jax.numpy as jnp
from jax import lax
from jax.experimental import pallas as pl
from jax.experimental.pallas import tpu as pltpu

F32 = jnp.float32
BF16 = jnp.bfloat16

D_MODEL = 2048
DEPTH = 4
PAGE = 128
CONV_W = 4
D_LRU = 1024
LRU_BLOCKS = 16
LRU_BW = 64
LRU_C = 8.0
H_SSM = 16
P_SSM = 64
D_SSM = 1024
G_SSM = 2
N_SSM = 128
D_XBC = D_SSM + 2 * G_SSM * N_SSM
H_ATT = 16
HD = 64
D_ATT = 1024
N_BRANCH = 3
D_FF = 4 * D_MODEL
EPS = 1e-6
NEG = -1e30

OFF_LRU = 0
OFF_Z = 2048
OFF_XBC = 3072
OFF_DT = 4608
OFF_QKV = 4624
OFF_F = 7696
OFF_GATES = 7712

VMEM_LIMIT = 56 * 1024 * 1024


def _cparams(sem):
    return pltpu.CompilerParams(dimension_semantics=sem, vmem_limit_bytes=VMEM_LIMIT)


def _dot(a, b):
    return jnp.dot(a, b, preferred_element_type=F32)


def _dot_nt(a, b):
    return lax.dot_general(a, b, (((1,), (1,)), ((), ())), preferred_element_type=F32)


def _split3(x):
    hi = x.astype(BF16)
    r = x - hi.astype(F32)
    mid = r.astype(BF16)
    lo = (r - mid.astype(F32)).astype(BF16)
    return hi, mid, lo


def _dot3_right(x, m01):
    hi, mid, lo = _split3(x)
    return _dot(hi, m01) + _dot(mid, m01) + _dot(lo, m01)


def _dot3_left(m01, x):
    hi, mid, lo = _split3(x)
    return _dot(m01, hi) + _dot(m01, mid) + _dot(m01, lo)


def _sigmoid(x):
    return 1.0 / (1.0 + jnp.exp(-x))


def _softplus(x):
    return jnp.maximum(x, 0.0) + jnp.log1p(jnp.exp(-jnp.abs(x)))


def _rms(x, g):
    ms = jnp.mean(x * x, axis=-1, keepdims=True)
    return x * lax.rsqrt(ms + EPS) * g


def _rms_kernel(x_ref, g_ref, o_ref):
    o_ref[...] = _rms(x_ref[...], g_ref[...]).astype(o_ref.dtype)


def _rmsnorm(x, g, bm):
    m, d = x.shape
    return pl.pallas_call(
        _rms_kernel,
        out_shape=jax.ShapeDtypeStruct((m, d), BF16),
        grid=(m // bm,),
        in_specs=[pl.BlockSpec((bm, d), lambda i: (i, 0)),
                  pl.BlockSpec((1, d), lambda i: (0, 0))],
        out_specs=pl.BlockSpec((bm, d), lambda i: (i, 0)),
        compiler_params=_cparams(("arbitrary",)),
        name="rmsnorm",
    )(x, g)


def _mm_nt_kernel(x_ref, xs_ref, w_ref, o_ref, os_ref, wb_ref):
    @pl.when(pl.program_id(1) == 0)
    def _():
        wb_ref[...] = w_ref[...].astype(BF16)
        os_ref[...] = _dot_nt(xs_ref[...], wb_ref[...])

    o_ref[...] = _dot_nt(x_ref[...], wb_ref[...])


def _mm_nt(x, xs, w_t, layer, off, n, bn, bm, name):
    m, k = x.shape
    ms = xs.shape[0]
    return pl.pallas_call(
        _mm_nt_kernel,
        out_shape=(jax.ShapeDtypeStruct((m, n), F32), jax.ShapeDtypeStruct((ms, n), F32)),
        grid=(n // bn, m // bm),
        in_specs=[pl.BlockSpec((bm, k), lambda j, i: (i, 0)),
                  pl.BlockSpec((ms, k), lambda j, i: (0, 0)),
                  pl.BlockSpec((None, pl.Element(bn), pl.Element(k)),
                               lambda j, i: (layer, pl.multiple_of(off + j * bn, 8), 0))],
        out_specs=(pl.BlockSpec((bm, bn), lambda j, i: (i, j)),
                   pl.BlockSpec((ms, bn), lambda j, i: (0, j))),
        scratch_shapes=[pltpu.VMEM((bn, k), BF16)],
        compiler_params=_cparams(("arbitrary", "arbitrary")),
        name=name,
    )(x, xs, w_t)


def _mm_small_kernel(x_ref, xs_ref, wa_ref, wb_ref, o_ref, os_ref):
    wa = wa_ref[...].astype(BF16)
    wb = wb_ref[...].astype(BF16)
    x = x_ref[...]
    o_ref[:, 0:128] = _dot_nt(x, wa)
    o_ref[:, 128:256] = _dot_nt(x, wb)

    @pl.when(pl.program_id(0) == 0)
    def _():
        xs = xs_ref[...]
        os_ref[:, 0:128] = _dot_nt(xs, wa)
        os_ref[:, 128:256] = _dot_nt(xs, wb)


def _mm_small(x, xs, w_t, layer, bm):
    m, k = x.shape
    ms = xs.shape[0]
    return pl.pallas_call(
        _mm_small_kernel,
        out_shape=(jax.ShapeDtypeStruct((m, 256), F32), jax.ShapeDtypeStruct((ms, 256), F32)),
        grid=(m // bm,),
        in_specs=[pl.BlockSpec((bm, k), lambda i: (i, 0)),
                  pl.BlockSpec((ms, k), lambda i: (0, 0)),
                  pl.BlockSpec((None, pl.Element(128), pl.Element(k)), lambda i: (layer, OFF_DT, 0)),
                  pl.BlockSpec((None, pl.Element(128), pl.Element(k)), lambda i: (layer, OFF_F, 0))],
        out_specs=(pl.BlockSpec((bm, 256), lambda i: (i, 0)),
                   pl.BlockSpec((ms, 256), lambda i: (0, 0))),
        compiler_params=_cparams(("arbitrary",)),
        name="inproj_small",
    )(x, xs, w_t, w_t)


def _causal_conv(ext_ref, x, cw, cb, rows):
    ext_ref[8:8 + rows, :] = x
    out = cb + ext_ref[5:5 + rows, :] * cw[0:1]
    out = out + ext_ref[6:6 + rows, :] * cw[1:2]
    out = out + ext_ref[7:7 + rows, :] * cw[2:3]
    out = out + x * cw[3:4]
    ext_ref[0:8, :] = ext_ref[rows:rows + 8, :]
    return out


def _lru_kernel(u_ref, g_ref, cw_ref, cb_ref, wa_ref, wx_ref, ba_ref, bx_ref, lam_ref,
                h0_ref, tail0_ref, y_ref, hl_ref, ext_ref, a_ref, b_ref, p_ref, hc_ref,
                *, tt, last_row):
    t = pl.program_id(1)
    nt = pl.num_programs(1)

    @pl.when(t == 0)
    def _():
        ext_ref[0:8, :] = tail0_ref[...]
        hc_ref[...] = h0_ref[...]

    uc = _causal_conv(ext_ref, u_ref[...], cw_ref[...], cb_ref[...], tt)
    ucb = uc.astype(BF16)
    ra = jnp.concatenate([_dot(ucb[:, 256 * c:256 * (c + 1)], wa_ref[c]) for c in range(4)], axis=1)
    rx = jnp.concatenate([_dot(ucb[:, 256 * c:256 * (c + 1)], wx_ref[c]) for c in range(4)], axis=1)
    r = _sigmoid(ra + ba_ref[...])
    gi = _sigmoid(rx + bx_ref[...])
    log_a = (-LRU_C * _softplus(-lam_ref[...])) * r
    a = jnp.exp(log_a)
    th = jnp.tanh(log_a)
    one_m_a2 = -2.0 * th / (1.0 - th)
    root = jnp.where(one_m_a2 > 0.0, one_m_a2 * lax.rsqrt(one_m_a2), 0.0)
    bvec = root * (gi * uc)
    nlc = D_LRU // 128
    for c in range(nlc):
        a_ref[c] = a[:, 128 * c:128 * (c + 1)]
        b_ref[c] = bvec[:, 128 * c:128 * (c + 1)]

    sub = tt // 8

    def step(j, carry):
        h, p = carry
        hs, ps = [], []
        for c in range(nlc):
            aa = a_ref[c, pl.ds(j, 8, stride=sub), :]
            bb = b_ref[c, pl.ds(j, 8, stride=sub), :]
            hc = aa * h[c] + bb
            pc = aa * p[c]
            b_ref[c, pl.ds(j, 8, stride=sub), :] = hc
            p_ref[c, pl.ds(j, 8, stride=sub), :] = pc
            hs.append(hc)
            ps.append(pc)
        return jnp.stack(hs), jnp.stack(ps)

    init = (jnp.zeros((nlc, 8, 128), F32), jnp.ones((nlc, 8, 128), F32))
    if sub == 1:
        hend, pend = step(0, init)
    else:
        hend, pend = lax.fori_loop(0, sub, step, init)
    hin = [hc_ref[...]]
    for s_ in range(8):
        he = jnp.concatenate([hend[c][s_:s_ + 1] for c in range(nlc)], axis=1)
        pe = jnp.concatenate([pend[c][s_:s_ + 1] for c in range(nlc)], axis=1)
        hin.append(he + pe * hin[s_])
    hc_ref[...] = hin[8]
    if sub == 1:
        hin_full = jnp.concatenate(hin[:8], axis=0)
    else:
        hin_full = jnp.concatenate([jnp.broadcast_to(hin[s_], (sub, D_LRU)) for s_ in range(8)], axis=0)
    h = jnp.concatenate([b_ref[c] + p_ref[c] * hin_full[:, 128 * c:128 * (c + 1)] for c in range(nlc)],
                        axis=1)
    g = g_ref[...]
    gelu = 0.5 * g * (1.0 + jnp.tanh(0.7978845608028654 * (g + 0.044715 * (g * g * g))))
    y_ref[...] = (h * gelu).astype(y_ref.dtype)

    @pl.when(t == nt - 1)
    def _():
        hl_ref[...] = h[last_row:last_row + 1]


def _lru_branch(lru, p, h0, tail0, nb, seq, tt, last_row):
    nt = seq // tt
    kern = functools.partial(_lru_kernel, tt=tt, last_row=last_row)
    vec = lambda: pl.BlockSpec((1, D_LRU), lambda b, t: (0, 0))
    return pl.pallas_call(
        kern,
        out_shape=(jax.ShapeDtypeStruct((nb * seq, D_LRU), BF16),
                   jax.ShapeDtypeStruct((nb, 1, D_LRU), F32)),
        grid=(nb, nt),
        in_specs=[pl.BlockSpec((tt, D_LRU), lambda b, t: (b * nt + t, 0)),
                  pl.BlockSpec((tt, D_LRU), lambda b, t: (b * nt + t, 1)),
                  pl.BlockSpec((CONV_W, D_LRU), lambda b, t: (0, 0)),
                  vec(),
                  pl.BlockSpec((4, 256, 256), lambda b, t: (0, 0, 0)),
                  pl.BlockSpec((4, 256, 256), lambda b, t: (0, 0, 0)),
                  vec(), vec(), vec(),
                  pl.BlockSpec((None, 1, D_LRU), lambda b, t: (b, 0, 0)),
                  pl.BlockSpec((None, 8, D_LRU), lambda b, t: (b, 0, 0))],
        out_specs=(pl.BlockSpec((tt, D_LRU), lambda b, t: (b * nt + t, 0)),
                   pl.BlockSpec((None, 1, D_LRU), lambda b, t: (b, 0, 0))),
        scratch_shapes=[pltpu.VMEM((tt + 8, D_LRU), F32),
                        pltpu.VMEM((D_LRU // 128, tt, 128), F32),
                        pltpu.VMEM((D_LRU // 128, tt, 128), F32),
                        pltpu.VMEM((D_LRU // 128, tt, 128), F32),
                        pltpu.VMEM((1, D_LRU), F32)],
        compiler_params=_cparams(("arbitrary", "arbitrary")),
        name="rglru",
    )(lru, lru, p["lru_conv_w"], p["lru_conv_b"], p["wa_bd"], p["wx_bd"],
      p["lru_ba"], p["lru_bx"], p["lru_lambda"], h0, tail0)


def _ssd_kernel(xbc_ref, z_ref, dtf_ref, cw_ref, cb_ref, dtb_ref, alog_ref, dx_ref, ng_ref,
                tri_ref, e_ref, st0_ref, tail0_ref, y_ref, stf_ref, ext_ref, st_ref, *, n_valid):
    c = pl.program_id(1)
    nc = pl.num_programs(1)
    q = PAGE

    @pl.when(c == 0)
    def _():
        ext_ref[0:8, :] = tail0_ref[...]
        st_ref[...] = st0_ref[...].T

    conv = _causal_conv(ext_ref, xbc_ref[...], cw_ref[...], cb_ref[...], q)
    xc = conv * _sigmoid(conv)
    xs = xc[:, 0:D_SSM]
    bm = xc[:, D_SSM:D_SSM + 256]
    cm = xc[:, D_SSM + 256:D_SSM + 512]

    row = lax.broadcasted_iota(jnp.int32, (q, q), 0)
    col = lax.broadcasted_iota(jnp.int32, (q, q), 1)
    dt = _softplus(dtf_ref[:, 0:128] + dtb_ref[...])
    if n_valid < q:
        dt = jnp.where(row < n_valid, dt, 0.0)
    a_neg = -jnp.exp(alog_ref[...])
    dta = dt * a_neg
    cs = _dot3_left(tri_ref[...], dta)
    cs_t = cs.T
    dt_t = dt.T
    cs_last = cs[q - 1:q, :]
    e01 = e_ref[...]
    exp_cs_x = _dot3_right(jnp.exp(cs), e01)
    ws_x = _dot3_right(jnp.exp(cs_last - cs) * dt, e01)
    dec_x = _dot3_right(jnp.broadcast_to(jnp.exp(cs_last), (8, 128)), e01)[0:1]
    dx = dx_ref[...]
    lane_lo = col < 64
    tril = col <= row

    ys = []
    for g in range(G_SSM):
        bg = bm[:, 128 * g:128 * (g + 1)]
        cgb = cm[:, 128 * g:128 * (g + 1)].astype(BF16)
        gl = slice(512 * g, 512 * (g + 1))
        cb_mat = _dot_nt(cgb, bg.astype(BF16))
        st_g = st_ref[:, gl]
        yoff = _dot(cgb, st_g.astype(BF16)) * exp_cs_x[:, gl]
        xw = (xs[:, gl] * ws_x[:, gl]).astype(BF16)
        st_ref[:, gl] = st_g * dec_x[:, gl] + _dot(bg.T.astype(BF16), xw)
        for jj in range(4):
            ws = []
            for a in range(2):
                h = 8 * g + 2 * jj + a
                seg = cs[:, h:h + 1] - cs_t[h:h + 1, :]
                lm = jnp.exp(jnp.where(tril, seg, NEG))
                ws.append((cb_mat * lm * dt_t[h:h + 1, :]).astype(BF16))
            wcat = jnp.concatenate(ws, axis=1)
            ll = slice(512 * g + 128 * jj, 512 * g + 128 * (jj + 1))
            x2 = xs[:, ll]
            xst = jnp.concatenate([jnp.where(lane_lo, x2, 0.0), jnp.where(lane_lo, 0.0, x2)],
                                  axis=0).astype(BF16)
            ys.append(_dot(wcat, xst) + yoff[:, 128 * jj:128 * (jj + 1)] + dx[:, ll] * x2)
    y = jnp.concatenate(ys, axis=1)
    z = z_ref[...]
    y_ref[...] = _rms(y * (z * _sigmoid(z)), ng_ref[...]).astype(y_ref.dtype)

    @pl.when(c == nc - 1)
    def _():
        stf_ref[...] = st_ref[...].T


def _ssd_branch(xbc, z, dtf, p, st0, tail0, nb, seq, n_valid):
    nc = seq // PAGE
    kern = functools.partial(_ssd_kernel, n_valid=n_valid)
    const = lambda shape: pl.BlockSpec(shape, lambda b, c: (0,) * len(shape))
    return pl.pallas_call(
        kern,
        out_shape=(jax.ShapeDtypeStruct((nb * seq, D_SSM), BF16),
                   jax.ShapeDtypeStruct((nb, D_SSM, N_SSM), F32)),
        grid=(nb, nc),
        in_specs=[pl.BlockSpec((PAGE, D_XBC), lambda b, c: (b * nc + c, 0)),
                  pl.BlockSpec((PAGE, D_SSM), lambda b, c: (b * nc + c, 0)),
                  pl.BlockSpec((PAGE, 256), lambda b, c: (b * nc + c, 0)),
                  const((CONV_W, D_XBC)), const((1, D_XBC)),
                  const((1, 128)), const((1, 128)), const((1, D_SSM)), const((1, D_SSM)),
                  const((128, 128)), const((128, D_SSM)),
                  pl.BlockSpec((None, D_SSM, N_SSM), lambda b, c: (b, 0, 0)),
                  pl.BlockSpec((None, 8, D_XBC), lambda b, c: (b, 0, 0))],
        out_specs=(pl.BlockSpec((PAGE, D_SSM), lambda b, c: (b * nc + c, 0)),
                   pl.BlockSpec((None, D_SSM, N_SSM), lambda b, c: (b, 0, 0))),
        scratch_shapes=[pltpu.VMEM((PAGE + 8, D_XBC), F32),
                        pltpu.VMEM((N_SSM, D_SSM), F32)],
        compiler_params=_cparams(("arbitrary", "arbitrary")),
        name="ssd",
    )(xbc, z, dtf, p["ssm_conv_w"], p["ssm_conv_b"], p["dt_bias128"], p["a_log128"],
      p["d_x"], p["ssm_norm_g"], p["tri_incl"], p["expand"], st0, tail0)


def _head_rms(x, bd, g):
    x2 = x * x
    hi = x2.astype(BF16)
    lo = (x2 - hi.astype(F32)).astype(BF16)
    ss = jnp.concatenate(
        [_dot(hi[:, 256 * c:256 * (c + 1)], bd) + _dot(lo[:, 256 * c:256 * (c + 1)], bd)
         for c in range(4)], axis=1)
    return x * lax.rsqrt(ss * (1.0 / HD) + EPS) * g


def _log_sigmoid(x):
    return -_softplus(-x)


AUG = 16
LOG2E = 1.4426950408889634


def _prep_kernel(*refs, aliased):
    (qkv_ref, dtf_ref, gq_ref, gk_ref, fb_ref, bd_ref, triu_ref, ones_ref, sel_ref, augc_ref) = refs[:10]
    refs = refs[10 + (3 if aliased else 0):]
    q_ref, kta_ref, vta_ref, kt_ref, vt_ref, lft_ref, car_ref = refs

    @pl.when(pl.program_id(1) == 0)
    def _():
        car_ref[...] = jnp.zeros_like(car_ref)

    bd = bd_ref[...]
    q = qkv_ref[:, 0:D_ATT]
    k = qkv_ref[:, D_ATT:2 * D_ATT]
    v = qkv_ref[:, 2 * D_ATT:3 * D_ATT]
    kt = _head_rms(k, bd, gk_ref[...]).T
    kt_ref[...] = kt
    vt = v.T
    vt_ref[...] = vt

    lf_t = _log_sigmoid(dtf_ref[:, 128:256] + fb_ref[...]).T
    lft_ref[...] = lf_t[0:H_ATT]
    f_t = _dot3_right(lf_t, triu_ref[...]) + car_ref[...]
    car_ref[...] = car_ref[...] + _dot3_right(lf_t, ones_ref[...])
    fs_t = f_t * LOG2E
    hi_t, mid_t, lo_t = _split3(fs_t)

    qn = _head_rms(q, bd, gq_ref[...]) * (HD ** -0.5 * LOG2E)
    lane_lo = lax.broadcasted_iota(jnp.int32, (PAGE, 128), 1) < 64
    blocks = []
    for m in range(H_ATT // 2):
        x2 = qn[:, 128 * m:128 * (m + 1)]
        blocks.append(jnp.where(lane_lo, x2, 0.0))
        blocks.append(jnp.where(lane_lo, pltpu.roll(x2, 64, 1), 0.0))
    parts = jnp.concatenate(_split3(fs_t.T), axis=1)
    q_ref[...] = (jnp.concatenate(blocks, axis=1) + _dot(parts, sel_ref[...]) + augc_ref[...]).astype(BF16)

    row = lax.broadcasted_iota(jnp.int32, (AUG, PAGE), 0)
    zpad = jnp.zeros((128 - HD - AUG, PAGE), BF16)
    v_aug = jnp.where(row == 0, 1.0, 0.0).astype(BF16)
    hi_f, mid_f, lo_f = hi_t.astype(F32), mid_t.astype(F32), lo_t.astype(F32)
    for h in range(H_ATT):
        k_aug = jnp.where(row < 3, 1.0,
                          jnp.where(row == 3, -hi_f[h:h + 1],
                                    jnp.where(row == 4, -mid_f[h:h + 1],
                                              jnp.where(row == 5, -lo_f[h:h + 1], 0.0)))).astype(BF16)
        kta_ref[h] = jnp.concatenate([kt[HD * h:HD * (h + 1)].astype(BF16), k_aug, zpad], axis=0)
        vta_ref[h] = jnp.concatenate([vt[HD * h:HD * (h + 1)].astype(BF16), v_aug, zpad], axis=0)


def _prep(qkv, dtf, p, nb, seq, layer, prev):
    npg = seq // PAGE
    m = nb * seq
    const = lambda shape: pl.BlockSpec(shape, lambda b, c: (0,) * len(shape))
    headpage = pl.BlockSpec((None, None, H_ATT, 128, PAGE), lambda b, c: (b, c, 0, 0, 0))
    stacked = lambda rows: pl.BlockSpec((None, None, None, rows, PAGE), lambda b, c: (layer, b, c, 0, 0))
    args = [qkv, dtf, p["gq_x"], p["gk_x"], p["f_bias128"], p["bd_ones"], p["tri_incl_t"], p["ones128"],
            p["sel_big"], p["aug_const"]]
    in_specs = [pl.BlockSpec((PAGE, 3 * D_ATT), lambda b, c: (b * npg + c, 0)),
                pl.BlockSpec((PAGE, 256), lambda b, c: (b * npg + c, 0)),
                const((1, D_ATT)), const((1, D_ATT)), const((1, 128)),
                const((256, 256)), const((128, 128)), const((128, 128)),
                const((3 * 128, H_ATT * 128)), const((1, H_ATT * 128))]
    aliases = {}
    if prev is not None:
        args += list(prev)
        in_specs += [pl.BlockSpec(memory_space=pl.ANY)] * 3
        aliases = {10: 3, 11: 4, 12: 5}
    return pl.pallas_call(
        functools.partial(_prep_kernel, aliased=prev is not None),
        out_shape=(jax.ShapeDtypeStruct((m, H_ATT * 128), BF16),
                   jax.ShapeDtypeStruct((nb, npg, H_ATT, 128, PAGE), BF16),
                   jax.ShapeDtypeStruct((nb, npg, H_ATT, 128, PAGE), BF16),
                   jax.ShapeDtypeStruct((DEPTH, nb, npg, D_ATT, PAGE), F32),
                   jax.ShapeDtypeStruct((DEPTH, nb, npg, D_ATT, PAGE), F32),
                   jax.ShapeDtypeStruct((DEPTH, nb, npg, H_ATT, PAGE), F32)),
        grid=(nb, npg),
        in_specs=in_specs,
        out_specs=(pl.BlockSpec((PAGE, H_ATT * 128), lambda b, c: (b * npg + c, 0)),
                   headpage, headpage, stacked(D_ATT), stacked(D_ATT), stacked(H_ATT)),
        scratch_shapes=[pltpu.VMEM((128, 128), F32)],
        input_output_aliases=aliases,
        compiler_params=_cparams(("arbitrary", "arbitrary")),
        name="attn_prep",
    )(*args)


def _attn_kernel(q_ref, kt_ref, vt_ref, o_ref, *, tq, pages):
    i = pl.program_id(2)
    tk = pages * PAGE
    assert tq == tk
    diag = (lax.broadcasted_iota(jnp.int32, (tq, tk), 1) <= lax.broadcasted_iota(jnp.int32, (tq, tk), 0))
    qs = [q_ref[:, 0:128], q_ref[:, 128:256]]

    def tile(kk, carry, masked):
        out = []
        for a in range(2):
            m_old, acc = carry[a]
            kt = jnp.concatenate([kt_ref[kk * pages + pg, a] for pg in range(pages)], axis=1)
            vt = jnp.concatenate([vt_ref[kk * pages + pg, a] for pg in range(pages)], axis=1)
            s = _dot(qs[a], kt)
            if masked:
                s = jnp.where(diag, s, NEG)
            m_new = jnp.maximum(m_old, jnp.max(s, axis=-1, keepdims=True))
            pr = jnp.exp2(s - m_new).astype(BF16)
            acc = acc * jnp.exp2(m_old - m_new) + _dot_nt(pr, vt)
            out.append((m_new, acc))
        return tuple(out)

    init = tuple((jnp.full((tq, 1), NEG, F32), jnp.zeros((tq, 128), F32)) for _ in range(2))
    carry = lax.fori_loop(0, i, lambda kk, c: tile(kk, c, False), init)
    (_, acc0), (_, acc1) = tile(i, carry, True)
    o0 = acc0 * (1.0 / acc0[:, HD:HD + 1])
    o1 = acc1 * (1.0 / acc1[:, HD:HD + 1])
    lane_lo = lax.broadcasted_iota(jnp.int32, (tq, 128), 1) < HD
    o_ref[...] = jnp.where(lane_lo, o0, pltpu.roll(o1, HD, 1)).astype(o_ref.dtype)


def _attention(q_aug, kta, vta, nb, seq, tq=512):
    nq = seq // tq
    npg = seq // PAGE
    kern = functools.partial(_attn_kernel, tq=tq, pages=tq // PAGE)
    return pl.pallas_call(
        kern,
        out_shape=jax.ShapeDtypeStruct((nb * seq, D_ATT), BF16),
        grid=(nb, H_ATT // 2, nq),
        in_specs=[pl.BlockSpec((tq, 256), lambda b, j, i: (b * nq + i, j)),
                  pl.BlockSpec((None, npg, 2, 128, PAGE), lambda b, j, i: (b, 0, j, 0, 0)),
                  pl.BlockSpec((None, npg, 2, 128, PAGE), lambda b, j, i: (b, 0, j, 0, 0))],
        out_specs=pl.BlockSpec((tq, 128), lambda b, j, i: (b * nq + i, j)),
        compiler_params=_cparams(("arbitrary", "arbitrary", "arbitrary")),
        name="fox_attention",
    )(q_aug, kta, vta)


def _merge_kernel(bg_ref, wb_ref, wo_ref, g2_ref,
                  y0_ref, y1_ref, y2_ref, gates_ref, x_ref,
                  sy0_ref, sy1_ref, sy2_ref, sgates_ref, sx_ref,
                  x1_ref, xn_ref, sx1_ref, sxn_ref):
    def rows(ys, gates, x, x1_out, xn_out):
        merged = None
        for n in range(N_BRANCH):
            pb = _dot(ys[n][...], wb_ref[n])
            gate = _sigmoid(gates[:, D_MODEL * n:D_MODEL * (n + 1)] + bg_ref[n:n + 1, :])
            merged = gate * pb if merged is None else merged + gate * pb
        x1 = x[...] + _dot(merged.astype(BF16), wo_ref[...])
        x1_out[...] = x1
        xn_out[...] = _rms(x1, g2_ref[...]).astype(xn_out.dtype)

    rows((y0_ref, y1_ref, y2_ref), gates_ref, x_ref, x1_ref, xn_ref)

    @pl.when(pl.program_id(0) == 0)
    def _():
        rows((sy0_ref, sy1_ref, sy2_ref), sgates_ref, sx_ref, sx1_ref, sxn_ref)


def _merge(prompt, sample, p, layer, bm):
    m = prompt[4].shape[0]
    ms = sample[4].shape[0]
    row = lambda w: pl.BlockSpec((bm, w), lambda i: (i, 0))
    srow = lambda w: pl.BlockSpec((ms, w), lambda i: (0, 0))
    widths = (D_LRU, D_SSM, D_ATT, N_BRANCH * D_MODEL, D_MODEL)
    return pl.pallas_call(
        _merge_kernel,
        out_shape=(jax.ShapeDtypeStruct((m, D_MODEL), F32),
                   jax.ShapeDtypeStruct((m, D_MODEL), BF16),
                   jax.ShapeDtypeStruct((ms, D_MODEL), F32),
                   jax.ShapeDtypeStruct((ms, D_MODEL), BF16)),
        grid=(m // bm,),
        in_specs=[pl.BlockSpec((N_BRANCH, D_MODEL), lambda i: (0, 0)),
                  pl.BlockSpec((None, N_BRANCH, D_LRU, D_MODEL), lambda i: (layer, 0, 0, 0),
                               pipeline_mode=pl.Buffered(1)),
                  pl.BlockSpec((None, D_MODEL, D_MODEL), lambda i: (layer, 0, 0),
                               pipeline_mode=pl.Buffered(1)),
                  pl.BlockSpec((1, D_MODEL), lambda i: (0, 0))]
                 + [row(w) for w in widths] + [srow(w) for w in widths],
        out_specs=(row(D_MODEL), row(D_MODEL), srow(D_MODEL), srow(D_MODEL)),
        compiler_params=_cparams(("arbitrary",)),
        name="merge_out",
    )(p["b_gate"], p["w_branch_bf"], p["w_out_bf"], p["norm2_g"], *prompt, *sample)


def _mlp_kernel(wu_ref, wd_ref, gn_ref, xn_ref, x1_ref, sxn_ref, sx1_ref, o_ref, on_ref, so_ref, son_ref):
    i = pl.program_id(0)
    f = pl.program_id(1)
    nf = pl.num_programs(1)

    def rows(xn, x1, out, out_n):
        @pl.when(f == 0)
        def _():
            out[...] = x1[...]

        h = jnp.maximum(_dot(xn[...], wu_ref[...]), 0.0)
        out[...] += _dot((h * h).astype(BF16), wd_ref[...])

        @pl.when(f == nf - 1)
        def _():
            out_n[...] = _rms(out[...], gn_ref[...]).astype(out_n.dtype)

    rows(xn_ref, x1_ref, o_ref, on_ref)

    @pl.when(i == 0)
    def _():
        rows(sxn_ref, sx1_ref, so_ref, son_ref)


def _mlp(prompt, sample, p, layer, g_next, bm, bf=512):
    m = prompt[1].shape[0]
    ms = sample[1].shape[0]
    once = pl.Buffered(1)
    row = lambda: pl.BlockSpec((bm, D_MODEL), lambda i, f: (i, 0), pipeline_mode=once)
    srow = lambda: pl.BlockSpec((ms, D_MODEL), lambda i, f: (0, 0))
    return pl.pallas_call(
        _mlp_kernel,
        out_shape=(jax.ShapeDtypeStruct((m, D_MODEL), F32),
                   jax.ShapeDtypeStruct((m, D_MODEL), BF16),
                   jax.ShapeDtypeStruct((ms, D_MODEL), F32),
                   jax.ShapeDtypeStruct((ms, D_MODEL), BF16)),
        grid=(m // bm, D_FF // bf),
        in_specs=[pl.BlockSpec((None, D_MODEL, bf), lambda i, f: (layer, 0, f)),
                  pl.BlockSpec((None, bf, D_MODEL), lambda i, f: (layer, f, 0)),
                  pl.BlockSpec((1, D_MODEL), lambda i, f: (0, 0)),
                  row(), row(), srow(), srow()],
        out_specs=(row(), row(), srow(), srow()),
        compiler_params=_cparams(("arbitrary", "arbitrary")),
        name="mlp",
    )(p["w_up_bf"], p["w_down_bf"], g_next, *prompt, *sample)


def _prep_s_kernel(qkv_ref, dtf_ref, gq_ref, gk_ref, fb_ref, bd_ref, q_ref, k_ref, lf_ref):
    bd = bd_ref[...]
    q_ref[...] = _head_rms(qkv_ref[:, 0:D_ATT], bd, gq_ref[...]) * (HD ** -0.5)
    k_ref[...] = _head_rms(qkv_ref[:, D_ATT:2 * D_ATT], bd, gk_ref[...])
    lf_ref[...] = _log_sigmoid(dtf_ref[:, 128:256] + fb_ref[...])


def _prep_s(qkv, dtf, p):
    nb = qkv.shape[0]
    full = lambda a: pl.BlockSpec(a.shape, lambda: (0,) * a.ndim)
    args = (qkv, dtf, p["gq_x"], p["gk_x"], p["f_bias128"], p["bd_ones"])
    return pl.pallas_call(
        _prep_s_kernel,
        out_shape=(jax.ShapeDtypeStruct((nb, D_ATT), F32),
                   jax.ShapeDtypeStruct((nb, D_ATT), F32),
                   jax.ShapeDtypeStruct((nb, 128), F32)),
        in_specs=[full(a) for a in args],
        out_specs=(pl.BlockSpec((nb, D_ATT), lambda: (0, 0)),
                   pl.BlockSpec((nb, D_ATT), lambda: (0, 0)),
                   pl.BlockSpec((nb, 128), lambda: (0, 0))),
        name="attn_prep_sample",
    )(*args)


def _dec_scores_kernel(pt_ref, qc_ref, *refs, group):
    k_refs = refs[:group]
    lf_refs = refs[group:2 * group]
    s_ref, lfo_ref = refs[2 * group:]
    qc = qc_ref[...]
    for g in range(group):
        prod = k_refs[g][...] * qc
        s_ref[g] = jnp.sum(prod.reshape(H_ATT, HD, PAGE), axis=1)
        lfo_ref[g] = lf_refs[g][...]


def _dec_scores(page_table, qcol, cache_kt, cache_lft, layer, group):
    nb, npg = page_table.shape
    k_specs = [pl.BlockSpec((None, None, D_ATT, PAGE),
                            functools.partial(lambda b, c, pt, g: (layer, pt[b, c * group + g], 0, 0), g=g))
               for g in range(group)]
    lf_specs = [pl.BlockSpec((None, None, H_ATT, PAGE),
                             functools.partial(lambda b, c, pt, g: (layer, pt[b, c * group + g], 0, 0), g=g))
                for g in range(group)]
    grid_spec = pltpu.PrefetchScalarGridSpec(
        num_scalar_prefetch=1,
        grid=(nb, npg // group),
        in_specs=[pl.BlockSpec((None, D_ATT, PAGE), lambda b, c, pt: (b, 0, 0))] + k_specs + lf_specs,
        out_specs=(pl.BlockSpec((None, group, H_ATT, PAGE), lambda b, c, pt: (b, c, 0, 0)),
                   pl.BlockSpec((None, group, H_ATT, PAGE), lambda b, c, pt: (b, c, 0, 0))),
    )
    return pl.pallas_call(
        functools.partial(_dec_scores_kernel, group=group),
        out_shape=(jax.ShapeDtypeStruct((nb, npg, H_ATT, PAGE), F32),
                   jax.ShapeDtypeStruct((nb, npg, H_ATT, PAGE), F32)),
        grid_spec=grid_spec,
        compiler_params=_cparams(("arbitrary", "arbitrary")),
        name="decode_scores",
    )(page_table, qcol, *([cache_kt] * group), *([cache_lft] * group))


def _dec_softmax_kernel(s_ref, lf_ref, qc_ref, kn_ref, lfn_ref, triu_ref, ones_ref,
                        p_ref, pn_ref, r_ref):
    npg = s_ref.shape[0]
    lf = lf_ref[...].reshape(npg * H_ATT, PAGE)
    r_ref[...] = _dot3_right(lf, triu_ref[...]).reshape(npg, H_ATT, PAGE)
    p_ref[...] = _dot3_right(lf, ones_ref[...]).reshape(npg, H_ATT, PAGE)

    def body(t, carry):
        pg = npg - 1 - t
        r_ref[pg] = r_ref[pg] + carry
        return carry + p_ref[pg]

    lax.fori_loop(0, npg, body, lfn_ref[...])
    s = s_ref[...] + r_ref[...]
    s_new = jnp.sum((qc_ref[...] * kn_ref[...]).reshape(H_ATT, HD, PAGE), axis=1)
    m = jnp.max(jnp.max(s, axis=0), axis=-1, keepdims=True)
    m = jnp.maximum(m, s_new)
    e = jnp.exp(s - m[None])
    e_new = jnp.exp(s_new - m)
    denom = jnp.sum(jnp.sum(e, axis=0), axis=-1, keepdims=True) + e_new
    inv = 1.0 / denom
    p_ref[...] = e * inv[None]
    pn_ref[...] = e_new * inv


def _dec_softmax(s, lfg, qcol, kncol, lfncol, p):
    nb, npg = s.shape[:2]
    blk4 = pl.BlockSpec((None, npg, H_ATT, PAGE), lambda b: (b, 0, 0, 0))
    col = pl.BlockSpec((None, D_ATT, PAGE), lambda b: (b, 0, 0))
    hrow = pl.BlockSpec((None, H_ATT, PAGE), lambda b: (b, 0, 0))
    c128 = pl.BlockSpec((128, 128), lambda b: (0, 0))
    return pl.pallas_call(
        _dec_softmax_kernel,
        out_shape=(jax.ShapeDtypeStruct((nb, npg, H_ATT, PAGE), F32),
                   jax.ShapeDtypeStruct((nb, H_ATT, PAGE), F32)),
        grid=(nb,),
        in_specs=[blk4, blk4, col, col, hrow, c128, c128],
        out_specs=(blk4, hrow),
        scratch_shapes=[pltpu.VMEM((npg, H_ATT, PAGE), F32)],
        compiler_params=_cparams(("arbitrary",)),
        name="decode_softmax",
    )(s, lfg, qcol, kncol, lfncol, p["tri_strict_t"], p["ones128"])


def _dec_values_kernel(pt_ref, p_ref, pn_ref, vn_ref, *refs, group):
    v_refs = refs[:group]
    o_ref = refs[group]
    acc_ref = refs[group + 1]
    c = pl.program_id(1)

    @pl.when(c == 0)
    def _():
        acc_ref[...] = jnp.zeros_like(acc_ref)

    acc = acc_ref[...]
    for g in range(group):
        pb = jnp.broadcast_to(p_ref[g][:, None, :], (H_ATT, HD, PAGE)).reshape(D_ATT, PAGE)
        acc = acc + v_refs[g][...] * pb
    acc_ref[...] = acc

    @pl.when(c == pl.num_programs(1) - 1)
    def _():
        pnb = jnp.broadcast_to(pn_ref[...][:, None, :], (H_ATT, HD, PAGE)).reshape(D_ATT, PAGE)
        o_ref[...] = jnp.sum(acc, axis=-1, keepdims=True) + pnb * vn_ref[...]


def _dec_values(page_table, pmat, pnew, vncol, cache_vt, layer, group):
    nb, npg = page_table.shape
    v_specs = [pl.BlockSpec((None, None, D_ATT, PAGE),
                            functools.partial(lambda b, c, pt, g: (layer, pt[b, c * group + g], 0, 0), g=g))
               for g in range(group)]
    grid_spec = pltpu.PrefetchScalarGridSpec(
        num_scalar_prefetch=1,
        grid=(nb, npg // group),
        in_specs=[pl.BlockSpec((None, group, H_ATT, PAGE), lambda b, c, pt: (b, c, 0, 0)),
                  pl.BlockSpec((None, H_ATT, PAGE), lambda b, c, pt: (b, 0, 0)),
                  pl.BlockSpec((None, D_ATT, PAGE), lambda b, c, pt: (b, 0, 0))] + v_specs,
        out_specs=pl.BlockSpec((None, D_ATT, PAGE), lambda b, c, pt: (b, 0, 0)),
        scratch_shapes=[pltpu.VMEM((D_ATT, PAGE), F32)],
    )
    return pl.pallas_call(
        functools.partial(_dec_values_kernel, group=group),
        out_shape=jax.ShapeDtypeStruct((nb, D_ATT, PAGE), F32),
        grid_spec=grid_spec,
        compiler_params=_cparams(("arbitrary", "arbitrary")),
        name="decode_values",
    )(page_table, pmat, pnew, vncol, *([cache_vt] * group))


def _block_diag_256(w):
    w4 = w.reshape(4, 4, LRU_BW, LRU_BW)
    eye = jnp.eye(4, dtype=w.dtype)
    bd = jnp.einsum("cjab,jk->cjakb", w4, eye)
    return bd.reshape(4, 256, 256).astype(BF16)


def _pad_lanes(v, width=128):
    return jnp.zeros((1, width), F32).at[0, :v.shape[0]].set(v)


def _constants():
    r = jnp.arange(128)
    tri_incl = (r[None, :] <= r[:, None]).astype(BF16)
    src = jnp.arange(3 * 128)
    dst = jnp.arange(H_ATT * 128)
    sel_big = ((dst[None, :] == 128 * (src[:, None] % 128) + HD + src[:, None] // 128)
               & (src[:, None] % 128 < H_ATT)).astype(BF16)
    aug_const = ((dst % 128 >= HD + 3) & (dst % 128 < HD + 6)).astype(F32)[None]
    return {
        "sel_big": sel_big,
        "aug_const": aug_const,
        "tri_incl": tri_incl,
        "tri_incl_t": tri_incl.T,
        "tri_strict_t": (r[:, None] > r[None, :]).astype(BF16),
        "ones128": jnp.ones((128, 128), BF16),
        "expand": (jnp.arange(D_SSM)[None, :] // P_SSM == r[:, None]).astype(BF16),
        "bd_ones": (jnp.arange(256)[:, None] // HD == jnp.arange(256)[None, :] // HD).astype(BF16),
    }


def _layer_params(l, a, consts):
    p = dict(consts)
    p["lru_conv_w"] = a["lru_conv_w"][l]
    p["lru_conv_b"] = a["lru_conv_b"][l][None]
    p["wa_bd"] = _block_diag_256(a["lru_wa"][l])
    p["wx_bd"] = _block_diag_256(a["lru_wx"][l])
    p["lru_ba"] = a["lru_ba"][l][None]
    p["lru_bx"] = a["lru_bx"][l][None]
    p["lru_lambda"] = a["lru_lambda"][l][None]
    p["ssm_conv_w"] = a["ssm_conv_w"][l]
    p["ssm_conv_b"] = a["ssm_conv_b"][l][None]
    p["dt_bias128"] = _pad_lanes(a["ssm_dt_bias"][l])
    p["a_log128"] = _pad_lanes(a["ssm_a_log"][l])
    p["d_x"] = jnp.repeat(a["ssm_d"][l], P_SSM)[None]
    p["ssm_norm_g"] = a["ssm_norm_g"][l][None]
    p["gq_x"] = jnp.tile(a["att_q_norm_g"][l], H_ATT)[None]
    p["gk_x"] = jnp.tile(a["att_k_norm_g"][l], H_ATT)[None]
    p["f_bias128"] = _pad_lanes(a["att_f_bias"][l])
    p["b_gate"] = a["b_gate"][l]
    p["norm2_g"] = a["norm2_g"][l][None]
    for name in ("w_branch_bf", "w_out_bf", "w_up_bf", "w_down_bf"):
        p[name] = a[name]
    return p


def _in_proj(xn, xsn, w_t, layer, bm):
    lru = _mm_nt(xn, xsn, w_t, layer, OFF_LRU, 2048, 1024, bm, "inproj_lru")
    z = _mm_nt(xn, xsn, w_t, layer, OFF_Z, 1024, 1024, bm, "inproj_z")
    xbc = _mm_nt(xn, xsn, w_t, layer, OFF_XBC, 1536, 768, bm, "inproj_xbc")
    qkv = _mm_nt(xn, xsn, w_t, layer, OFF_QKV, 3072, 1024, bm, "inproj_qkv")
    gates = _mm_nt(xn, xsn, w_t, layer, OFF_GATES, 6144, 1024, bm, "inproj_gates")
    dtf = _mm_small(xn, xsn, w_t, layer, bm)
    return lru, z, xbc, qkv, gates, dtf


def _pad_rows(x, rows):
    nb, w = x.shape
    return jnp.zeros((nb, rows, w), x.dtype).at[:, 0].set(x).reshape(nb * rows, w)


def _tail8(buf):
    return jnp.pad(buf, ((0, 0), (5, 0), (0, 0)))


def kernel(x_prompt, x_sample, cache_k, cache_v, cache_logf, state_lru_h, state_lru_conv, state_ssm, state_ssm_conv, page_table, norm1_g, w_in, b_gate, lru_conv_w, lru_conv_b, lru_wa, lru_ba, lru_wx, lru_bx, lru_lambda, ssm_conv_w, ssm_conv_b, ssm_dt_bias, ssm_a_log, ssm_d, ssm_norm_g, att_q_norm_g, att_k_norm_g, att_f_bias, w_branch, w_out, norm2_g, w_up, w_down):
    a = dict(b_gate=b_gate, lru_conv_w=lru_conv_w, lru_conv_b=lru_conv_b, lru_wa=lru_wa, lru_ba=lru_ba,
             lru_wx=lru_wx, lru_bx=lru_bx, lru_lambda=lru_lambda, ssm_conv_w=ssm_conv_w,
             ssm_conv_b=ssm_conv_b, ssm_dt_bias=ssm_dt_bias, ssm_a_log=ssm_a_log, ssm_d=ssm_d,
             ssm_norm_g=ssm_norm_g, att_q_norm_g=att_q_norm_g, att_k_norm_g=att_k_norm_g,
             att_f_bias=att_f_bias, norm2_g=norm2_g,
             w_branch_bf=w_branch.astype(BF16), w_out_bf=w_out.astype(BF16),
             w_up_bf=w_up.astype(BF16), w_down_bf=w_down.astype(BF16))
    nbp, seq, _ = x_prompt.shape
    nbs = x_sample.shape[0]
    mp = nbp * seq
    npg = seq // PAGE
    n_pool = cache_k.shape[1]
    consts = _constants()

    w_t = jnp.swapaxes(w_in, 1, 2)
    cache_kt = jnp.transpose(cache_k, (0, 1, 3, 4, 2)).reshape(DEPTH, n_pool, D_ATT, PAGE)
    cache_vt = jnp.transpose(cache_v, (0, 1, 3, 4, 2)).reshape(DEPTH, n_pool, D_ATT, PAGE)
    cache_lft = jnp.transpose(cache_logf, (0, 1, 3, 2))

    xp = x_prompt.reshape(mp, D_MODEL)
    xs = x_sample.reshape(nbs, D_MODEL)
    xpn = _rmsnorm(xp, norm1_g[0][None], 512)
    xsn = _rmsnorm(xs, norm1_g[0][None], nbs)

    zeros_h = jnp.zeros((nbp, 1, D_LRU), F32)
    zeros_lt = jnp.zeros((nbp, 8, D_LRU), F32)
    zeros_st = jnp.zeros((nbp, D_SSM, N_SSM), F32)
    zeros_xt = jnp.zeros((nbp, 8, D_XBC), F32)

    outs = [[] for _ in range(14)]
    kv_prev = None
    for l in range(DEPTH):
        p = _layer_params(l, a, consts)
        g_next = norm1_g[(l + 1) % DEPTH][None]

        ((lru, lru_s), (z, z_s), (xbc, xbc_s), (qkv, qkv_s), (gates, gates_s), (dtf, dtf_s)) = _in_proj(
            xpn, xsn, w_t, l, min(1024, mp))

        y_lru, h_last = _lru_branch(lru, p, zeros_h, zeros_lt, nbp, seq, 512, 511)
        y_ssm, st_last = _ssd_branch(xbc, z, dtf, p, zeros_st, zeros_xt, nbp, seq, PAGE)
        q_aug, kta, vta, kt_all, vt_all, lft_all = _prep(qkv, dtf, p, nbp, seq, l, kv_prev)
        kv_prev = (kt_all, vt_all, lft_all)
        y_att = _attention(q_aug, kta, vta, nbp, seq)
        outs[6].append(h_last.reshape(nbp, D_LRU))
        outs[8].append(lru.reshape(nbp, seq, 2 * D_LRU)[:, seq - 3:, :D_LRU])
        outs[10].append(st_last.reshape(nbp, H_SSM, P_SSM, N_SSM))
        outs[12].append(xbc.reshape(nbp, seq, D_XBC)[:, seq - 3:, :])

        y_lru_s, h_s = _lru_branch(_pad_rows(lru_s, 8), p, state_lru_h[l][:, None, :],
                                   _tail8(state_lru_conv[l]), nbs, 8, 8, 0)
        y_ssm_s, st_s = _ssd_branch(_pad_rows(xbc_s, PAGE), _pad_rows(z_s, PAGE), _pad_rows(dtf_s, PAGE), p,
                                    state_ssm[l].reshape(nbs, D_SSM, N_SSM), _tail8(state_ssm_conv[l]),
                                    nbs, PAGE, 1)
        qs, ks, lfs = _prep_s(qkv_s, dtf_s, p)
        vs = qkv_s[:, 2 * D_ATT:]
        bcast = lambda v: jnp.broadcast_to(v[:, :, None], (nbs, v.shape[1], PAGE))
        qcol = bcast(qs)
        s_mat, lfg = _dec_scores(page_table, qcol, cache_kt, cache_lft, l, 16)
        pmat, pnew = _dec_softmax(s_mat, lfg, qcol, bcast(ks), bcast(lfs[:, :H_ATT]), p)
        o_s = _dec_values(page_table, pmat, pnew, bcast(vs), cache_vt, l, 16)
        y_att_s = o_s[:, :, 0].astype(BF16)

        x1, x1n, x1s, x1sn = _merge((y_lru, y_ssm, y_att, gates, xp),
                                    (y_lru_s[::8], y_ssm_s[::PAGE], y_att_s, gates_s, xs), p, l, 256)
        xp, xpn, xs, xsn = _mlp((x1n, x1), (x1sn, x1s), p, l, g_next, min(1024, mp))
        outs[3].append(ks.reshape(nbs, 1, H_ATT, HD))
        outs[4].append(vs.reshape(nbs, 1, H_ATT, HD))
        outs[5].append(lfs[:, :H_ATT].reshape(nbs, 1, H_ATT))
        outs[7].append(h_s.reshape(nbs, D_LRU))
        outs[9].append(jnp.concatenate([state_lru_conv[l][:, 1:], lru_s[:, None, :D_LRU]], axis=1))
        outs[11].append(st_s.reshape(nbs, H_SSM, P_SSM, N_SSM))
        outs[13].append(jnp.concatenate([state_ssm_conv[l][:, 1:], xbc_s[:, None, :]], axis=1))

    st = [jnp.stack(o) if o else None for o in outs]
    kt_all, vt_all, lft_all = kv_prev
    new_k_p = jnp.transpose(kt_all.reshape(DEPTH, nbp, npg, H_ATT, HD, PAGE), (0, 1, 2, 5, 3, 4))
    new_v_p = jnp.transpose(vt_all.reshape(DEPTH, nbp, npg, H_ATT, HD, PAGE), (0, 1, 2, 5, 3, 4))
    new_lf_p = jnp.transpose(lft_all, (0, 1, 2, 4, 3))
    return (xp.reshape(nbp, seq, D_MODEL), xs.reshape(nbs, 1, D_MODEL),
            new_k_p, new_v_p, new_lf_p, st[3], st[4], st[5],
            st[6], st[7], st[8], st[9], st[10], st[11], st[12], st[13])
```

```python
import functools

import jax
import jax.numpy as jnp
from jax import lax
from jax.experimental import pallas as pl
from jax.experimental.pallas import tpu as pltpu

F32 = jnp.float32
BF16 = jnp.bfloat16

D_MODEL = 2048
DEPTH = 4
PAGE = 128
CONV_W = 4
D_LRU = 1024
LRU_BLOCKS = 16
LRU_BW = 64
LRU_C = 8.0
H_SSM = 16
P_SSM = 64
D_SSM = 1024
G_SSM = 2
N_SSM = 128
D_XBC = D_SSM + 2 * G_SSM * N_SSM
H_ATT = 16
HD = 64
D_ATT = 1024
N_BRANCH = 3
D_FF = 4 * D_MODEL
EPS = 1e-6
NEG = -1e30

OFF_LRU = 0
OFF_Z = 2048
OFF_XBC = 3072
OFF_DT = 4608
OFF_QKV = 4624
OFF_F = 7696
OFF_GATES = 7712

VMEM_LIMIT = 56 * 1024 * 1024


def _cparams(sem):
    return pltpu.CompilerParams(dimension_semantics=sem, vmem_limit_bytes=VMEM_LIMIT)


def _dot(a, b):
    return jnp.dot(a, b, preferred_element_type=F32)


def _dot_nt(a, b):
    return lax.dot_general(a, b, (((1,), (1,)), ((), ())), preferred_element_type=F32)


def _split3(x):
    hi = x.astype(BF16)
    r = x - hi.astype(F32)
    mid = r.astype(BF16)
    lo = (r - mid.astype(F32)).astype(BF16)
    return hi, mid, lo


def _dot3_right(x, m01):
    hi, mid, lo = _split3(x)
    return _dot(hi, m01) + _dot(mid, m01) + _dot(lo, m01)


def _dot3_left(m01, x):
    hi, mid, lo = _split3(x)
    return _dot(m01, hi) + _dot(m01, mid) + _dot(m01, lo)


def _sigmoid(x):
    return 0.5 * jnp.tanh(0.5 * x) + 0.5


def _softplus(x):
    return jnp.maximum(x, 0.0) + jnp.log1p(jnp.exp(-jnp.abs(x)))


def _rms(x, g):
    ms = jnp.mean(x * x, axis=-1, keepdims=True)
    return x * lax.rsqrt(ms + EPS) * g


def _rms_kernel(x_ref, g_ref, o_ref):
    o_ref[...] = _rms(x_ref[...], g_ref[...]).astype(o_ref.dtype)


def _rmsnorm(x, g, bm):
    m, d = x.shape
    return pl.pallas_call(
        _rms_kernel,
        out_shape=jax.ShapeDtypeStruct((m, d), BF16),
        grid=(m // bm,),
        in_specs=[pl.BlockSpec((bm, d), lambda i: (i, 0)),
                  pl.BlockSpec((1, d), lambda i: (0, 0))],
        out_specs=pl.BlockSpec((bm, d), lambda i: (i, 0)),
        compiler_params=_cparams(("arbitrary",)),
        name="rmsnorm",
    )(x, g)


def _mm_nt_kernel(x_ref, xs_ref, w_ref, o_ref, os_ref, wb_ref):
    @pl.when(pl.program_id(1) == 0)
    def _():
        wb_ref[...] = w_ref[...].astype(BF16)
        os_ref[...] = _dot_nt(xs_ref[...], wb_ref[...])

    o_ref[...] = _dot_nt(x_ref[...], wb_ref[...])


def _mm_nt(x, xs, w_t, layer, off, n, bn, bm, name):
    m, k = x.shape
    ms = xs.shape[0]
    return pl.pallas_call(
        _mm_nt_kernel,
        out_shape=(jax.ShapeDtypeStruct((m, n), F32), jax.ShapeDtypeStruct((ms, n), F32)),
        grid=(n // bn, m // bm),
        in_specs=[pl.BlockSpec((bm, k), lambda j, i: (i, 0)),
                  pl.BlockSpec((ms, k), lambda j, i: (0, 0)),
                  pl.BlockSpec((None, pl.Element(bn), pl.Element(k)),
                               lambda j, i: (layer, pl.multiple_of(off + j * bn, 8), 0))],
        out_specs=(pl.BlockSpec((bm, bn), lambda j, i: (i, j)),
                   pl.BlockSpec((ms, bn), lambda j, i: (0, j))),
        scratch_shapes=[pltpu.VMEM((bn, k), BF16)],
        compiler_params=_cparams(("arbitrary", "arbitrary")),
        name=name,
    )(x, xs, w_t)


def _mm_small_kernel(x_ref, xs_ref, wa_ref, wb_ref, o_ref, os_ref):
    wa = wa_ref[...].astype(BF16)
    wb = wb_ref[...].astype(BF16)
    x = x_ref[...]
    o_ref[:, 0:128] = _dot_nt(x, wa)
    o_ref[:, 128:256] = _dot_nt(x, wb)

    @pl.when(pl.program_id(0) == 0)
    def _():
        xs = xs_ref[...]
        os_ref[:, 0:128] = _dot_nt(xs, wa)
        os_ref[:, 128:256] = _dot_nt(xs, wb)


def _mm_small(x, xs, w_t, layer, bm):
    m, k = x.shape
    ms = xs.shape[0]
    return pl.pallas_call(
        _mm_small_kernel,
        out_shape=(jax.ShapeDtypeStruct((m, 256), F32), jax.ShapeDtypeStruct((ms, 256), F32)),
        grid=(m // bm,),
        in_specs=[pl.BlockSpec((bm, k), lambda i: (i, 0)),
                  pl.BlockSpec((ms, k), lambda i: (0, 0)),
                  pl.BlockSpec((None, pl.Element(128), pl.Element(k)), lambda i: (layer, OFF_DT, 0)),
                  pl.BlockSpec((None, pl.Element(128), pl.Element(k)), lambda i: (layer, OFF_F, 0))],
        out_specs=(pl.BlockSpec((bm, 256), lambda i: (i, 0)),
                   pl.BlockSpec((ms, 256), lambda i: (0, 0))),
        compiler_params=_cparams(("arbitrary",)),
        name="inproj_small",
    )(x, xs, w_t, w_t)


def _causal_conv(ext_ref, x, cw, cb, rows):
    ext_ref[8:8 + rows, :] = x
    out = cb + ext_ref[5:5 + rows, :] * cw[0:1]
    out = out + ext_ref[6:6 + rows, :] * cw[1:2]
    out = out + ext_ref[7:7 + rows, :] * cw[2:3]
    out = out + x * cw[3:4]
    ext_ref[0:8, :] = ext_ref[rows:rows + 8, :]
    return out


def _lru_kernel(u_ref, g_ref, cw_ref, cb_ref, wa_ref, wx_ref, ba_ref, bx_ref, lam_ref,
                h0_ref, tail0_ref, y_ref, hl_ref, ext_ref, hc_ref,
                *, tt, last_row):
    t = pl.program_id(1)
    nt = pl.num_programs(1)

    @pl.when(t == 0)
    def _():
        ext_ref[0:8, :] = tail0_ref[...]
        hc_ref[...] = h0_ref[...]

    uc = _causal_conv(ext_ref, u_ref[...], cw_ref[...], cb_ref[...], tt)
    ucb = uc.astype(BF16)
    ra = jnp.concatenate([_dot(ucb[:, 256 * c:256 * (c + 1)], wa_ref[c]) for c in range(4)], axis=1)
    rx = jnp.concatenate([_dot(ucb[:, 256 * c:256 * (c + 1)], wx_ref[c]) for c in range(4)], axis=1)
    r = _sigmoid(ra + ba_ref[...])
    gi = _sigmoid(rx + bx_ref[...])
    log_a = (-LRU_C * _softplus(-lam_ref[...])) * r
    a = jnp.exp(log_a)
    th = jnp.tanh(log_a)
    one_m_a2 = -2.0 * th / (1.0 - th)
    root = jnp.where(one_m_a2 > 0.0, one_m_a2 * lax.rsqrt(one_m_a2), 0.0)
    bvec = root * (gi * uc)

    row8 = lax.broadcasted_iota(jnp.int32, (tt, D_LRU), 0) & 7
    for k in (1, 2, 4):
        inside = row8 >= k
        a_prev = jnp.where(inside, pltpu.roll(a, k, 0), 1.0)
        b_prev = jnp.where(inside, pltpu.roll(bvec, k, 0), 0.0)
        bvec = bvec + a * b_prev
        a = a * a_prev
    h_in = hc_ref[...]
    groups = []
    for r in range(tt // 8):
        hg = bvec[8 * r:8 * (r + 1)] + a[8 * r:8 * (r + 1)] * h_in
        groups.append(hg)
        h_in = hg[7:8]
    hc_ref[...] = h_in
    h = jnp.concatenate(groups, axis=0)
    g = g_ref[...]
    gelu = 0.5 * g * (1.0 + jnp.tanh(0.7978845608028654 * (g + 0.044715 * (g * g * g))))
    y_ref[...] = (h * gelu).astype(y_ref.dtype)

    @pl.when(t == nt - 1)
    def _():
        hl_ref[...] = h[last_row:last_row + 1]


def _lru_branch(lru, p, h0, tail0, nb, seq, tt, last_row):
    nt = seq // tt
    kern = functools.partial(_lru_kernel, tt=tt, last_row=last_row)
    vec = lambda: pl.BlockSpec((1, D_LRU), lambda b, t: (0, 0))
    return pl.pallas_call(
        kern,
        out_shape=(jax.ShapeDtypeStruct((nb * seq, D_LRU), BF16),
                   jax.ShapeDtypeStruct((nb, 1, D_LRU), F32)),
        grid=(nb, nt),
        in_specs=[pl.BlockSpec((tt, D_LRU), lambda b, t: (b * nt + t, 0)),
                  pl.BlockSpec((tt, D_LRU), lambda b, t: (b * nt + t, 1)),
                  pl.BlockSpec((CONV_W, D_LRU), lambda b, t: (0, 0)),
                  vec(),
                  pl.BlockSpec((4, 256, 256), lambda b, t: (0, 0, 0)),
                  pl.BlockSpec((4, 256, 256), lambda b, t: (0, 0, 0)),
                  vec(), vec(), vec(),
                  pl.BlockSpec((None, 1, D_LRU), lambda b, t: (b, 0, 0)),
                  pl.BlockSpec((None, 8, D_LRU), lambda b, t: (b, 0, 0))],
        out_specs=(pl.BlockSpec((tt, D_LRU), lambda b, t: (b * nt + t, 0)),
                   pl.BlockSpec((None, 1, D_LRU), lambda b, t: (b, 0, 0))),
        scratch_shapes=[pltpu.VMEM((tt + 8, D_LRU), F32),
                        pltpu.VMEM((1, D_LRU), F32)],
        compiler_params=_cparams(("arbitrary", "arbitrary")),
        name="rglru",
    )(lru, lru, p["lru_conv_w"], p["lru_conv_b"], p["wa_bd"], p["wx_bd"],
      p["lru_ba"], p["lru_bx"], p["lru_lambda"], h0, tail0)


def _ssd_kernel(*refs, n_valid, ride):
    if ride:
        refs = refs[1:]
        _decode_scores_pages(refs[13], refs[14:14 + ride], refs[14 + ride:14 + 2 * ride],
                             refs[16 + 2 * ride], refs[17 + 2 * ride])
        refs = refs[:13] + refs[14 + 2 * ride:16 + 2 * ride] + refs[18 + 2 * ride:]
    (xbc_ref, z_ref, dtf_ref, cw_ref, cb_ref, dtb_ref, alog_ref, dx_ref, ng_ref,
     tri_ref, e_ref, st0_ref, tail0_ref, y_ref, stf_ref, ext_ref, st_ref) = refs
    c = pl.program_id(1)
    nc = pl.num_programs(1)
    q = PAGE

    @pl.when(c == 0)
    def _():
        ext_ref[0:8, :] = tail0_ref[...]
        st_ref[...] = st0_ref[...].T

    conv = _causal_conv(ext_ref, xbc_ref[...], cw_ref[...], cb_ref[...], q)
    xc = conv * _sigmoid(conv)
    xs = xc[:, 0:D_SSM]
    bm = xc[:, D_SSM:D_SSM + 256]
    cm = xc[:, D_SSM + 256:D_SSM + 512]

    row = lax.broadcasted_iota(jnp.int32, (q, q), 0)
    col = lax.broadcasted_iota(jnp.int32, (q, q), 1)
    dt = _softplus(dtf_ref[:, 0:128] + dtb_ref[...])
    if n_valid < q:
        dt = jnp.where(row < n_valid, dt, 0.0)
    a_neg = -jnp.exp(alog_ref[...])
    dta = dt * a_neg
    cs = _dot3_left(tri_ref[...], dta)
    cs_t = cs.T
    dt_t = dt.T
    cs_last = cs[q - 1:q, :]
    e01 = e_ref[...]
    exp_cs_x = _dot3_right(jnp.exp(cs), e01)
    ws_x = _dot3_right(jnp.exp(cs_last - cs) * dt, e01)
    dec_x = _dot3_right(jnp.broadcast_to(jnp.exp(cs_last), (8, 128)), e01)[0:1]
    dx = dx_ref[...]
    lane_lo = col < 64
    tril = col <= row

    ys = []
    for g in range(G_SSM):
        bg = bm[:, 128 * g:128 * (g + 1)]
        cgb = cm[:, 128 * g:128 * (g + 1)].astype(BF16)
        gl = slice(512 * g, 512 * (g + 1))
        cb_mat = _dot_nt(cgb, bg.astype(BF16))
        st_g = st_ref[:, gl]
        yoff = _dot(cgb, st_g.astype(BF16)) * exp_cs_x[:, gl]
        xw = (xs[:, gl] * ws_x[:, gl]).astype(BF16)
        st_ref[:, gl] = st_g * dec_x[:, gl] + _dot(bg.T.astype(BF16), xw)
        for jj in range(4):
            ws = []
            for a in range(2):
                h = 8 * g + 2 * jj + a
                seg = cs[:, h:h + 1] - cs_t[h:h + 1, :]
                lm = jnp.exp(jnp.where(tril, seg, NEG))
                ws.append((cb_mat * lm * dt_t[h:h + 1, :]).astype(BF16))
            wcat = jnp.concatenate(ws, axis=1)
            ll = slice(512 * g + 128 * jj, 512 * g + 128 * (jj + 1))
            x2 = xs[:, ll]
            xst = jnp.concatenate([jnp.where(lane_lo, x2, 0.0), jnp.where(lane_lo, 0.0, x2)],
                                  axis=0).astype(BF16)
            ys.append(_dot(wcat, xst) + yoff[:, 128 * jj:128 * (jj + 1)] + dx[:, ll] * x2)
    y = jnp.concatenate(ys, axis=1)
    z = z_ref[...]
    y_ref[...] = _rms(y * (z * _sigmoid(z)), ng_ref[...]).astype(y_ref.dtype)

    @pl.when(c == nc - 1)
    def _():
        stf_ref[...] = st_ref[...].T


def _rider_pages(page_table, n_steps):
    nbs, npg = page_table.shape
    steps_per_row = n_steps // nbs
    assert steps_per_row * nbs == n_steps and npg % steps_per_row == 0
    return steps_per_row, npg // steps_per_row


def _ssd_branch(xbc, z, dtf, p, st0, tail0, nb, seq, n_valid, decode=None):
    nc = seq // PAGE
    const = lambda shape: pl.BlockSpec(shape, lambda b, c, *_: (0,) * len(shape))
    in_specs = [pl.BlockSpec((PAGE, D_XBC), lambda b, c, *_: (b * nc + c, 0)),
                pl.BlockSpec((PAGE, D_SSM), lambda b, c, *_: (b * nc + c, 0)),
                pl.BlockSpec((PAGE, 256), lambda b, c, *_: (b * nc + c, 0)),
                const((CONV_W, D_XBC)), const((1, D_XBC)),
                const((1, 128)), const((1, 128)), const((1, D_SSM)), const((1, D_SSM)),
                const((128, 128)), const((128, D_SSM)),
                pl.BlockSpec((None, D_SSM, N_SSM), lambda b, c, *_: (b, 0, 0)),
                pl.BlockSpec((None, 8, D_XBC), lambda b, c, *_: (b, 0, 0))]
    out_specs = [pl.BlockSpec((PAGE, D_SSM), lambda b, c, *_: (b * nc + c, 0)),
                 pl.BlockSpec((None, D_SSM, N_SSM), lambda b, c, *_: (b, 0, 0))]
    out_shape = [jax.ShapeDtypeStruct((nb * seq, D_SSM), BF16),
                 jax.ShapeDtypeStruct((nb, D_SSM, N_SSM), F32)]
    args = [xbc, z, dtf, p["ssm_conv_w"], p["ssm_conv_b"], p["dt_bias128"], p["a_log128"],
            p["d_x"], p["ssm_norm_g"], p["tri_incl"], p["expand"], st0, tail0]
    scratch = [pltpu.VMEM((PAGE + 8, D_XBC), F32), pltpu.VMEM((N_SSM, D_SSM), F32)]
    ride = 0
    if decode is not None:
        page_table, qcol, cache_kt, cache_lft, layer = decode
        nbs, npg = page_table.shape
        spr, ride = _rider_pages(page_table, nb * nc)
        row = lambda b, c: (b * nc + c) // spr
        grp = lambda b, c: (b * nc + c) % spr
        pg = lambda g: (lambda b, c, pt: (layer, pt[row(b, c), grp(b, c) * ride + g], 0, 0))
        in_specs += [pl.BlockSpec((None, D_ATT, PAGE), lambda b, c, pt: (row(b, c), 0, 0))]
        in_specs += [pl.BlockSpec((None, None, D_ATT, PAGE), pg(g)) for g in range(ride)]
        in_specs += [pl.BlockSpec((None, None, H_ATT, PAGE), pg(g)) for g in range(ride)]
        out_specs += [pl.BlockSpec((None, ride, H_ATT, PAGE), lambda b, c, pt: (row(b, c), grp(b, c), 0, 0))] * 2
        out_shape += [jax.ShapeDtypeStruct((nbs, npg, H_ATT, PAGE), F32)] * 2
        args = [page_table] + args + [qcol] + [cache_kt] * ride + [cache_lft] * ride
    grid_spec = pltpu.PrefetchScalarGridSpec(
        num_scalar_prefetch=1 if decode is not None else 0,
        grid=(nb, nc), in_specs=in_specs, out_specs=tuple(out_specs), scratch_shapes=scratch)
    return pl.pallas_call(
        functools.partial(_ssd_kernel, n_valid=n_valid, ride=ride),
        out_shape=tuple(out_shape),
        grid_spec=grid_spec,
        compiler_params=_cparams(("arbitrary", "arbitrary")),
        name="ssd",
    )(*args)


def _head_rms(x, bd, g):
    x2 = x * x
    hi = x2.astype(BF16)
    lo = (x2 - hi.astype(F32)).astype(BF16)
    ss = jnp.concatenate(
        [_dot(hi[:, 256 * c:256 * (c + 1)], bd) + _dot(lo[:, 256 * c:256 * (c + 1)], bd)
         for c in range(4)], axis=1)
    return x * lax.rsqrt(ss * (1.0 / HD) + EPS) * g


def _log_sigmoid(x):
    return -_softplus(-x)


AUG = 16
LOG2E = 1.4426950408889634


def _prep_kernel(*refs, aliased):
    (qkv_ref, dtf_ref, gq_ref, gk_ref, fb_ref, bd_ref, triu_ref, ones_ref, sel_ref, augc_ref) = refs[:10]
    refs = refs[10 + (3 if aliased else 0):]
    q_ref, kta_ref, vta_ref, kt_ref, vt_ref, lft_ref, car_ref = refs

    @pl.when(pl.program_id(1) == 0)
    def _():
        car_ref[...] = jnp.zeros_like(car_ref)

    bd = bd_ref[...]
    q = qkv_ref[:, 0:D_ATT]
    k = qkv_ref[:, D_ATT:2 * D_ATT]
    v = qkv_ref[:, 2 * D_ATT:3 * D_ATT]
    kt = _head_rms(k, bd, gk_ref[...]).T
    kt_ref[...] = kt
    vt = v.T
    vt_ref[...] = vt

    lf_t = _log_sigmoid(dtf_ref[:, 128:256] + fb_ref[...]).T
    lft_ref[...] = lf_t[0:H_ATT]
    f_t = _dot3_right(lf_t, triu_ref[...]) + car_ref[...]
    car_ref[...] = car_ref[...] + _dot3_right(lf_t, ones_ref[...])
    fs_t = f_t * LOG2E
    hi_t, mid_t, lo_t = _split3(fs_t)

    qn = _head_rms(q, bd, gq_ref[...]) * (HD ** -0.5 * LOG2E)
    lane_lo = lax.broadcasted_iota(jnp.int32, (PAGE, 128), 1) < 64
    blocks = []
    for m in range(H_ATT // 2):
        x2 = qn[:, 128 * m:128 * (m + 1)]
        blocks.append(jnp.where(lane_lo, x2, 0.0))
        blocks.append(jnp.where(lane_lo, pltpu.roll(x2, 64, 1), 0.0))
    parts = jnp.concatenate(_split3(fs_t.T), axis=1)
    q_ref[...] = (jnp.concatenate(blocks, axis=1) + _dot(parts, sel_ref[...]) + augc_ref[...]).astype(BF16)

    row = lax.broadcasted_iota(jnp.int32, (AUG, PAGE), 0)
    zpad = jnp.zeros((128 - HD - AUG, PAGE), BF16)
    v_aug = jnp.where(row == 0, 1.0, 0.0).astype(BF16)
    hi_f, mid_f, lo_f = hi_t.astype(F32), mid_t.astype(F32), lo_t.astype(F32)
    for h in range(H_ATT):
        k_aug = jnp.where(row < 3, 1.0,
                          jnp.where(row == 3, -hi_f[h:h + 1],
                                    jnp.where(row == 4, -mid_f[h:h + 1],
                                              jnp.where(row == 5, -lo_f[h:h + 1], 0.0)))).astype(BF16)
        kta_ref[h] = jnp.concatenate([kt[HD * h:HD * (h + 1)].astype(BF16), k_aug, zpad], axis=0)
        vta_ref[h] = jnp.concatenate([vt[HD * h:HD * (h + 1)].astype(BF16), v_aug, zpad], axis=0)


def _prep(qkv, dtf, p, nb, seq, layer, prev):
    npg = seq // PAGE
    m = nb * seq
    const = lambda shape: pl.BlockSpec(shape, lambda b, c: (0,) * len(shape))
    headpage = pl.BlockSpec((None, None, H_ATT, 128, PAGE), lambda b, c: (b, c, 0, 0, 0))
    stacked = lambda rows: pl.BlockSpec((None, None, None, rows, PAGE), lambda b, c: (layer, b, c, 0, 0))
    args = [qkv, dtf, p["gq_x"], p["gk_x"], p["f_bias128"], p["bd_ones"], p["tri_incl_t"], p["ones128"],
            p["sel_big"], p["aug_const"]]
    in_specs = [pl.BlockSpec((PAGE, 3 * D_ATT), lambda b, c: (b * npg + c, 0)),
                pl.BlockSpec((PAGE, 256), lambda b, c: (b * npg + c, 0)),
                const((1, D_ATT)), const((1, D_ATT)), const((1, 128)),
                const((256, 256)), const((128, 128)), const((128, 128)),
                const((3 * 128, H_ATT * 128)), const((1, H_ATT * 128))]
    aliases = {}
    if prev is not None:
        args += list(prev)
        in_specs += [pl.BlockSpec(memory_space=pl.ANY)] * 3
        aliases = {10: 3, 11: 4, 12: 5}
    return pl.pallas_call(
        functools.partial(_prep_kernel, aliased=prev is not None),
        out_shape=(jax.ShapeDtypeStruct((m, H_ATT * 128), BF16),
                   jax.ShapeDtypeStruct((nb, npg, H_ATT, 128, PAGE), BF16),
                   jax.ShapeDtypeStruct((nb, npg, H_ATT, 128, PAGE), BF16),
                   jax.ShapeDtypeStruct((DEPTH, nb, npg, D_ATT, PAGE), F32),
                   jax.ShapeDtypeStruct((DEPTH, nb, npg, D_ATT, PAGE), F32),
                   jax.ShapeDtypeStruct((DEPTH, nb, npg, H_ATT, PAGE), F32)),
        grid=(nb, npg),
        in_specs=in_specs,
        out_specs=(pl.BlockSpec((PAGE, H_ATT * 128), lambda b, c: (b * npg + c, 0)),
                   headpage, headpage, stacked(D_ATT), stacked(D_ATT), stacked(H_ATT)),
        scratch_shapes=[pltpu.VMEM((128, 128), F32)],
        input_output_aliases=aliases,
        compiler_params=_cparams(("arbitrary", "arbitrary")),
        name="attn_prep",
    )(*args)


def _attn_kernel(*refs, tq, pages, ride, steps_per_row):
    if ride:
        q_ref, kt_ref, vt_ref, p_ref, pn_ref, vn_ref = refs[1:7]
        v_refs = refs[7:7 + ride]
        o_ref, od_ref, acc_ref = refs[7 + ride:]
        step = ((pl.program_id(0) * pl.num_programs(1) + pl.program_id(1)) * pl.num_programs(2)
                + pl.program_id(2))
        grp = step % steps_per_row
        _decode_values_pages(p_ref, pn_ref, vn_ref, v_refs, od_ref, acc_ref,
                             grp == 0, grp == steps_per_row - 1)
    else:
        q_ref, kt_ref, vt_ref, o_ref = refs
    i = pl.program_id(2)
    tk = pages * PAGE
    assert tq == tk
    diag = (lax.broadcasted_iota(jnp.int32, (tq, tk), 1) <= lax.broadcasted_iota(jnp.int32, (tq, tk), 0))
    qs = [q_ref[:, 0:128], q_ref[:, 128:256]]

    def tile(kk, carry, masked):
        out = []
        for a in range(2):
            m_old, acc = carry[a]
            kt = jnp.concatenate([kt_ref[kk * pages + pg, a] for pg in range(pages)], axis=1)
            vt = jnp.concatenate([vt_ref[kk * pages + pg, a] for pg in range(pages)], axis=1)
            s = _dot(qs[a], kt)
            if masked:
                s = jnp.where(diag, s, NEG)
            m_new = jnp.maximum(m_old, jnp.max(s, axis=-1, keepdims=True))
            pr = jnp.exp2(s - m_new).astype(BF16)
            acc = acc * jnp.exp2(m_old - m_new) + _dot_nt(pr, vt)
            out.append((m_new, acc))
        return tuple(out)

    init = tuple((jnp.full((tq, 1), NEG, F32), jnp.zeros((tq, 128), F32)) for _ in range(2))
    carry = lax.fori_loop(0, i, lambda kk, c: tile(kk, c, False), init)
    (_, acc0), (_, acc1) = tile(i, carry, True)
    o0 = acc0 * (1.0 / acc0[:, HD:HD + 1])
    o1 = acc1 * (1.0 / acc1[:, HD:HD + 1])
    lane_lo = lax.broadcasted_iota(jnp.int32, (tq, 128), 1) < HD
    o_ref[...] = jnp.where(lane_lo, o0, pltpu.roll(o1, HD, 1)).astype(o_ref.dtype)


def _attention(q_aug, kta, vta, nb, seq, tq=512, decode=None):
    nq = seq // tq
    npg = seq // PAGE
    nj = H_ATT // 2
    in_specs = [pl.BlockSpec((tq, 256), lambda b, j, i, *_: (b * nq + i, j)),
                pl.BlockSpec((None, npg, 2, 128, PAGE), lambda b, j, i, *_: (b, 0, j, 0, 0)),
                pl.BlockSpec((None, npg, 2, 128, PAGE), lambda b, j, i, *_: (b, 0, j, 0, 0))]
    out_specs = [pl.BlockSpec((tq, 128), lambda b, j, i, *_: (b * nq + i, j))]
    out_shape = [jax.ShapeDtypeStruct((nb * seq, D_ATT), BF16)]
    args = [q_aug, kta, vta]
    scratch = []
    ride, spr = 0, 1
    if decode is not None:
        page_table, pmat, pnew, vncol, cache_vt, layer = decode
        nbs = page_table.shape[0]
        spr, ride = _rider_pages(page_table, nb * nj * nq)
        step = lambda b, j, i: (b * nj + j) * nq + i
        row = lambda b, j, i: step(b, j, i) // spr
        grp = lambda b, j, i: step(b, j, i) % spr
        pg = lambda g: (lambda b, j, i, pt: (layer, pt[row(b, j, i), grp(b, j, i) * ride + g], 0, 0))
        in_specs += [pl.BlockSpec((None, ride, H_ATT, PAGE), lambda b, j, i, pt: (row(b, j, i), grp(b, j, i), 0, 0)),
                     pl.BlockSpec((None, H_ATT, PAGE), lambda b, j, i, pt: (row(b, j, i), 0, 0)),
                     pl.BlockSpec((None, D_ATT, PAGE), lambda b, j, i, pt: (row(b, j, i), 0, 0))]
        in_specs += [pl.BlockSpec((None, None, D_ATT, PAGE), pg(g)) for g in range(ride)]
        out_specs += [pl.BlockSpec((None, D_ATT, PAGE), lambda b, j, i, pt: (row(b, j, i), 0, 0))]
        out_shape += [jax.ShapeDtypeStruct((nbs, D_ATT, PAGE), F32)]
        args = [page_table] + args + [pmat, pnew, vncol] + [cache_vt] * ride
        scratch = [pltpu.VMEM((D_ATT, PAGE), F32)]
    grid_spec = pltpu.PrefetchScalarGridSpec(
        num_scalar_prefetch=1 if decode is not None else 0,
        grid=(nb, nj, nq), in_specs=in_specs, out_specs=tuple(out_specs), scratch_shapes=scratch)
    out = pl.pallas_call(
        functools.partial(_attn_kernel, tq=tq, pages=tq // PAGE, ride=ride, steps_per_row=spr),
        out_shape=tuple(out_shape),
        grid_spec=grid_spec,
        compiler_params=_cparams(("arbitrary", "arbitrary", "arbitrary")),
        name="fox_attention",
    )(*args)
    return out if decode is not None else out[0]


def _merge_kernel(bg_ref, wb_ref, wo_ref, g2_ref,
                  y0_ref, y1_ref, y2_ref, gates_ref, x_ref,
                  sy0_ref, sy1_ref, sy2_ref, sgates_ref, sx_ref,
                  x1_ref, xn_ref, sx1_ref, sxn_ref):
    def rows(ys, gates, x, x1_out, xn_out):
        merged = None
        for n in range(N_BRANCH):
            pb = _dot(ys[n][...], wb_ref[n])
            gate = _sigmoid(gates[:, D_MODEL * n:D_MODEL * (n + 1)] + bg_ref[n:n + 1, :])
            merged = gate * pb if merged is None else merged + gate * pb
        x1 = x[...] + _dot(merged.astype(BF16), wo_ref[...])
        x1_out[...] = x1
        xn_out[...] = _rms(x1, g2_ref[...]).astype(xn_out.dtype)

    rows((y0_ref, y1_ref, y2_ref), gates_ref, x_ref, x1_ref, xn_ref)

    @pl.when(pl.program_id(0) == 0)
    def _():
        rows((sy0_ref, sy1_ref, sy2_ref), sgates_ref, sx_ref, sx1_ref, sxn_ref)


def _merge(prompt, sample, p, layer, bm):
    m = prompt[4].shape[0]
    ms = sample[4].shape[0]
    row = lambda w: pl.BlockSpec((bm, w), lambda i: (i, 0))
    srow = lambda w: pl.BlockSpec((ms, w), lambda i: (0, 0))
    widths = (D_LRU, D_SSM, D_ATT, N_BRANCH * D_MODEL, D_MODEL)
    return pl.pallas_call(
        _merge_kernel,
        out_shape=(jax.ShapeDtypeStruct((m, D_MODEL), F32),
                   jax.ShapeDtypeStruct((m, D_MODEL), BF16),
                   jax.ShapeDtypeStruct((ms, D_MODEL), F32),
                   jax.ShapeDtypeStruct((ms, D_MODEL), BF16)),
        grid=(m // bm,),
        in_specs=[pl.BlockSpec((N_BRANCH, D_MODEL), lambda i: (0, 0)),
                  pl.BlockSpec((None, N_BRANCH, D_LRU, D_MODEL), lambda i: (layer, 0, 0, 0),
                               pipeline_mode=pl.Buffered(1)),
                  pl.BlockSpec((None, D_MODEL, D_MODEL), lambda i: (layer, 0, 0),
                               pipeline_mode=pl.Buffered(1)),
                  pl.BlockSpec((1, D_MODEL), lambda i: (0, 0))]
                 + [row(w) for w in widths] + [srow(w) for w in widths],
        out_specs=(row(D_MODEL), row(D_MODEL), srow(D_MODEL), srow(D_MODEL)),
        compiler_params=_cparams(("arbitrary",)),
        name="merge_out",
    )(p["b_gate"], p["w_branch_bf"], p["w_out_bf"], p["norm2_g"], *prompt, *sample)


def _mlp_kernel(wu_ref, wd_ref, gn_ref, xn_ref, x1_ref, sxn_ref, sx1_ref, o_ref, on_ref, so_ref, son_ref):
    i = pl.program_id(0)
    f = pl.program_id(1)
    nf = pl.num_programs(1)

    def rows(xn, x1, out, out_n):
        @pl.when(f == 0)
        def _():
            out[...] = x1[...]

        h = jnp.maximum(_dot(xn[...], wu_ref[...]), 0.0)
        out[...] += _dot((h * h).astype(BF16), wd_ref[...])

        @pl.when(f == nf - 1)
        def _():
            out_n[...] = _rms(out[...], gn_ref[...]).astype(out_n.dtype)

    rows(xn_ref, x1_ref, o_ref, on_ref)

    @pl.when(i == 0)
    def _():
        rows(sxn_ref, sx1_ref, so_ref, son_ref)


def _mlp(prompt, sample, p, layer, g_next, bm, bf=512):
    m = prompt[1].shape[0]
    ms = sample[1].shape[0]
    once = pl.Buffered(1)
    row = lambda: pl.BlockSpec((bm, D_MODEL), lambda i, f: (i, 0), pipeline_mode=once)
    srow = lambda: pl.BlockSpec((ms, D_MODEL), lambda i, f: (0, 0))
    return pl.pallas_call(
        _mlp_kernel,
        out_shape=(jax.ShapeDtypeStruct((m, D_MODEL), F32),
                   jax.ShapeDtypeStruct((m, D_MODEL), BF16),
                   jax.ShapeDtypeStruct((ms, D_MODEL), F32),
                   jax.ShapeDtypeStruct((ms, D_MODEL), BF16)),
        grid=(m // bm, D_FF // bf),
        in_specs=[pl.BlockSpec((None, D_MODEL, bf), lambda i, f: (layer, 0, f)),
                  pl.BlockSpec((None, bf, D_MODEL), lambda i, f: (layer, f, 0)),
                  pl.BlockSpec((1, D_MODEL), lambda i, f: (0, 0)),
                  row(), row(), srow(), srow()],
        out_specs=(row(), row(), srow(), srow()),
        compiler_params=_cparams(("arbitrary", "arbitrary")),
        name="mlp",
    )(p["w_up_bf"], p["w_down_bf"], g_next, *prompt, *sample)


def _prep_s_kernel(qkv_ref, dtf_ref, gq_ref, gk_ref, fb_ref, bd_ref, q_ref, k_ref, lf_ref):
    bd = bd_ref[...]
    q_ref[...] = _head_rms(qkv_ref[:, 0:D_ATT], bd, gq_ref[...]) * (HD ** -0.5)
    k_ref[...] = _head_rms(qkv_ref[:, D_ATT:2 * D_ATT], bd, gk_ref[...])
    lf_ref[...] = _log_sigmoid(dtf_ref[:, 128:256] + fb_ref[...])


def _prep_s(qkv, dtf, p):
    nb = qkv.shape[0]
    full = lambda a: pl.BlockSpec(a.shape, lambda: (0,) * a.ndim)
    args = (qkv, dtf, p["gq_x"], p["gk_x"], p["f_bias128"], p["bd_ones"])
    return pl.pallas_call(
        _prep_s_kernel,
        out_shape=(jax.ShapeDtypeStruct((nb, D_ATT), F32),
                   jax.ShapeDtypeStruct((nb, D_ATT), F32),
                   jax.ShapeDtypeStruct((nb, 128), F32)),
        in_specs=[full(a) for a in args],
        out_specs=(pl.BlockSpec((nb, D_ATT), lambda: (0, 0)),
                   pl.BlockSpec((nb, D_ATT), lambda: (0, 0)),
                   pl.BlockSpec((nb, 128), lambda: (0, 0))),
        name="attn_prep_sample",
    )(*args)


def _decode_scores_pages(qc_ref, k_refs, lf_refs, s_ref, lfo_ref):
    qc = qc_ref[...]
    for g in range(len(k_refs)):
        prod = k_refs[g][...] * qc
        s_ref[g] = jnp.sum(prod.reshape(H_ATT, HD, PAGE), axis=1)
        lfo_ref[g] = lf_refs[g][...]


def _decode_values_pages(p_ref, pn_ref, vn_ref, v_refs, o_ref, acc_ref, first, last):
    @pl.when(first)
    def _():
        acc_ref[...] = jnp.zeros_like(acc_ref)

    acc = acc_ref[...]
    for g in range(len(v_refs)):
        pb = jnp.broadcast_to(p_ref[g][:, None, :], (H_ATT, HD, PAGE)).reshape(D_ATT, PAGE)
        acc = acc + v_refs[g][...] * pb
    acc_ref[...] = acc

    @pl.when(last)
    def _():
        pnb = jnp.broadcast_to(pn_ref[...][:, None, :], (H_ATT, HD, PAGE)).reshape(D_ATT, PAGE)
        o_ref[...] = jnp.sum(acc, axis=-1, keepdims=True) + pnb * vn_ref[...]


def _dec_softmax_kernel(s_ref, lf_ref, qc_ref, kn_ref, lfn_ref, triu_ref, ones_ref,
                        p_ref, pn_ref, r_ref):
    npg = s_ref.shape[0]
    lf = lf_ref[...].reshape(npg * H_ATT, PAGE)
    r_ref[...] = _dot3_right(lf, triu_ref[...]).reshape(npg, H_ATT, PAGE)
    p_ref[...] = _dot3_right(lf, ones_ref[...]).reshape(npg, H_ATT, PAGE)

    def body(t, carry):
        pg = npg - 1 - t
        r_ref[pg] = r_ref[pg] + carry
        return carry + p_ref[pg]

    lax.fori_loop(0, npg, body, lfn_ref[...])
    s = s_ref[...] + r_ref[...]
    s_new = jnp.sum((qc_ref[...] * kn_ref[...]).reshape(H_ATT, HD, PAGE), axis=1)
    m = jnp.max(jnp.max(s, axis=0), axis=-1, keepdims=True)
    m = jnp.maximum(m, s_new)
    e = jnp.exp(s - m[None])
    e_new = jnp.exp(s_new - m)
    denom = jnp.sum(jnp.sum(e, axis=0), axis=-1, keepdims=True) + e_new
    inv = 1.0 / denom
    p_ref[...] = e * inv[None]
    pn_ref[...] = e_new * inv


def _dec_softmax(s, lfg, qcol, kncol, lfncol, p):
    nb, npg = s.shape[:2]
    blk4 = pl.BlockSpec((None, npg, H_ATT, PAGE), lambda b: (b, 0, 0, 0))
    col = pl.BlockSpec((None, D_ATT, PAGE), lambda b: (b, 0, 0))
    hrow = pl.BlockSpec((None, H_ATT, PAGE), lambda b: (b, 0, 0))
    c128 = pl.BlockSpec((128, 128), lambda b: (0, 0))
    return pl.pallas_call(
        _dec_softmax_kernel,
        out_shape=(jax.ShapeDtypeStruct((nb, npg, H_ATT, PAGE), F32),
                   jax.ShapeDtypeStruct((nb, H_ATT, PAGE), F32)),
        grid=(nb,),
        in_specs=[blk4, blk4, col, col, hrow, c128, c128],
        out_specs=(blk4, hrow),
        scratch_shapes=[pltpu.VMEM((npg, H_ATT, PAGE), F32)],
        compiler_params=_cparams(("arbitrary",)),
        name="decode_softmax",
    )(s, lfg, qcol, kncol, lfncol, p["tri_strict_t"], p["ones128"])


def _block_diag_256(w):
    w4 = w.reshape(4, 4, LRU_BW, LRU_BW)
    eye = jnp.eye(4, dtype=w.dtype)
    bd = jnp.einsum("cjab,jk->cjakb", w4, eye)
    return bd.reshape(4, 256, 256).astype(BF16)


def _pad_lanes(v, width=128):
    return jnp.zeros((1, width), F32).at[0, :v.shape[0]].set(v)


def _constants():
    r = jnp.arange(128)
    tri_incl = (r[None, :] <= r[:, None]).astype(BF16)
    src = jnp.arange(3 * 128)
    dst = jnp.arange(H_ATT * 128)
    sel_big = ((dst[None, :] == 128 * (src[:, None] % 128) + HD + src[:, None] // 128)
               & (src[:, None] % 128 < H_ATT)).astype(BF16)
    aug_const = ((dst % 128 >= HD + 3) & (dst % 128 < HD + 6)).astype(F32)[None]
    return {
        "sel_big": sel_big,
        "aug_const": aug_const,
        "tri_incl": tri_incl,
        "tri_incl_t": tri_incl.T,
        "tri_strict_t": (r[:, None] > r[None, :]).astype(BF16),
        "ones128": jnp.ones((128, 128), BF16),
        "expand": (jnp.arange(D_SSM)[None, :] // P_SSM == r[:, None]).astype(BF16),
        "bd_ones": (jnp.arange(256)[:, None] // HD == jnp.arange(256)[None, :] // HD).astype(BF16),
    }


def _layer_params(l, a, consts):
    p = dict(consts)
    p["lru_conv_w"] = a["lru_conv_w"][l]
    p["lru_conv_b"] = a["lru_conv_b"][l][None]
    p["wa_bd"] = _block_diag_256(a["lru_wa"][l])
    p["wx_bd"] = _block_diag_256(a["lru_wx"][l])
    p["lru_ba"] = a["lru_ba"][l][None]
    p["lru_bx"] = a["lru_bx"][l][None]
    p["lru_lambda"] = a["lru_lambda"][l][None]
    p["ssm_conv_w"] = a["ssm_conv_w"][l]
    p["ssm_conv_b"] = a["ssm_conv_b"][l][None]
    p["dt_bias128"] = _pad_lanes(a["ssm_dt_bias"][l])
    p["a_log128"] = _pad_lanes(a["ssm_a_log"][l])
    p["d_x"] = jnp.repeat(a["ssm_d"][l], P_SSM)[None]
    p["ssm_norm_g"] = a["ssm_norm_g"][l][None]
    p["gq_x"] = jnp.tile(a["att_q_norm_g"][l], H_ATT)[None]
    p["gk_x"] = jnp.tile(a["att_k_norm_g"][l], H_ATT)[None]
    p["f_bias128"] = _pad_lanes(a["att_f_bias"][l])
    p["b_gate"] = a["b_gate"][l]
    p["norm2_g"] = a["norm2_g"][l][None]
    for name in ("w_branch_bf", "w_out_bf", "w_up_bf", "w_down_bf"):
        p[name] = a[name]
    return p


def _in_proj(xn, xsn, w_t, layer, bm):
    lru = _mm_nt(xn, xsn, w_t, layer, OFF_LRU, 2048, 1024, bm, "inproj_lru")
    z = _mm_nt(xn, xsn, w_t, layer, OFF_Z, 1024, 1024, bm, "inproj_z")
    xbc = _mm_nt(xn, xsn, w_t, layer, OFF_XBC, 1536, 768, bm, "inproj_xbc")
    qkv = _mm_nt(xn, xsn, w_t, layer, OFF_QKV, 3072, 1024, bm, "inproj_qkv")
    gates = _mm_nt(xn, xsn, w_t, layer, OFF_GATES, 6144, 1024, bm, "inproj_gates")
    dtf = _mm_small(xn, xsn, w_t, layer, bm)
    return lru, z, xbc, qkv, gates, dtf


def _pad_rows(x, rows):
    nb, w = x.shape
    return jnp.zeros((nb, rows, w), x.dtype).at[:, 0].set(x).reshape(nb * rows, w)


def _tail8(buf):
    return jnp.pad(buf, ((0, 0), (5, 0), (0, 0)))


def kernel(x_prompt, x_sample, cache_k, cache_v, cache_logf, state_lru_h, state_lru_conv, state_ssm, state_ssm_conv, page_table, norm1_g, w_in, b_gate, lru_conv_w, lru_conv_b, lru_wa, lru_ba, lru_wx, lru_bx, lru_lambda, ssm_conv_w, ssm_conv_b, ssm_dt_bias, ssm_a_log, ssm_d, ssm_norm_g, att_q_norm_g, att_k_norm_g, att_f_bias, w_branch, w_out, norm2_g, w_up, w_down):
    a = dict(b_gate=b_gate, lru_conv_w=lru_conv_w, lru_conv_b=lru_conv_b, lru_wa=lru_wa, lru_ba=lru_ba,
             lru_wx=lru_wx, lru_bx=lru_bx, lru_lambda=lru_lambda, ssm_conv_w=ssm_conv_w,
             ssm_conv_b=ssm_conv_b, ssm_dt_bias=ssm_dt_bias, ssm_a_log=ssm_a_log, ssm_d=ssm_d,
             ssm_norm_g=ssm_norm_g, att_q_norm_g=att_q_norm_g, att_k_norm_g=att_k_norm_g,
             att_f_bias=att_f_bias, norm2_g=norm2_g,
             w_branch_bf=w_branch.astype(BF16), w_out_bf=w_out.astype(BF16),
             w_up_bf=w_up.astype(BF16), w_down_bf=w_down.astype(BF16))
    nbp, seq, _ = x_prompt.shape
    nbs = x_sample.shape[0]
    mp = nbp * seq
    npg = seq // PAGE
    n_pool = cache_k.shape[1]
    consts = _constants()

    w_t = jnp.swapaxes(w_in, 1, 2)
    cache_kt = jnp.transpose(cache_k, (0, 1, 3, 4, 2)).reshape(DEPTH, n_pool, D_ATT, PAGE)
    cache_vt = jnp.transpose(cache_v, (0, 1, 3, 4, 2)).reshape(DEPTH, n_pool, D_ATT, PAGE)
    cache_lft = jnp.transpose(cache_logf, (0, 1, 3, 2))

    xp = x_prompt.reshape(mp, D_MODEL)
    xs = x_sample.reshape(nbs, D_MODEL)
    xpn = _rmsnorm(xp, norm1_g[0][None], 512)
    xsn = _rmsnorm(xs, norm1_g[0][None], nbs)

    zeros_h = jnp.zeros((nbp, 1, D_LRU), F32)
    zeros_lt = jnp.zeros((nbp, 8, D_LRU), F32)
    zeros_st = jnp.zeros((nbp, D_SSM, N_SSM), F32)
    zeros_xt = jnp.zeros((nbp, 8, D_XBC), F32)

    outs = [[] for _ in range(14)]
    kv_prev = None
    for l in range(DEPTH):
        p = _layer_params(l, a, consts)
        g_next = norm1_g[(l + 1) % DEPTH][None]

        ((lru, lru_s), (z, z_s), (xbc, xbc_s), (qkv, qkv_s), (gates, gates_s), (dtf, dtf_s)) = _in_proj(
            xpn, xsn, w_t, l, min(1024, mp))

        qs, ks, lfs = _prep_s(qkv_s, dtf_s, p)
        vs = qkv_s[:, 2 * D_ATT:]
        bcast = lambda v: jnp.broadcast_to(v[:, :, None], (nbs, v.shape[1], PAGE))
        qcol = bcast(qs)
        y_lru, h_last = _lru_branch(lru, p, zeros_h, zeros_lt, nbp, seq, 512, 511)
        y_ssm, st_last, s_mat, lfg = _ssd_branch(xbc, z, dtf, p, zeros_st, zeros_xt, nbp, seq, PAGE,
                                                 decode=(page_table, qcol, cache_kt, cache_lft, l))
        pmat, pnew = _dec_softmax(s_mat, lfg, qcol, bcast(ks), bcast(lfs[:, :H_ATT]), p)
        q_aug, kta, vta, kt_all, vt_all, lft_all = _prep(qkv, dtf, p, nbp, seq, l, kv_prev)
        kv_prev = (kt_all, vt_all, lft_all)
        y_att, o_s = _attention(q_aug, kta, vta, nbp, seq,
                                decode=(page_table, pmat, pnew, bcast(vs), cache_vt, l))
        y_att_s = o_s[:, :, 0].astype(BF16)
        outs[6].append(h_last.reshape(nbp, D_LRU))
        outs[8].append(lru.reshape(nbp, seq, 2 * D_LRU)[:, seq - 3:, :D_LRU])
        outs[10].append(st_last.reshape(nbp, H_SSM, P_SSM, N_SSM))
        outs[12].append(xbc.reshape(nbp, seq, D_XBC)[:, seq - 3:, :])

        y_lru_s, h_s = _lru_branch(_pad_rows(lru_s, 8), p, state_lru_h[l][:, None, :],
                                   _tail8(state_lru_conv[l]), nbs, 8, 8, 0)
        y_ssm_s, st_s = _ssd_branch(_pad_rows(xbc_s, PAGE), _pad_rows(z_s, PAGE), _pad_rows(dtf_s, PAGE), p,
                                    state_ssm[l].reshape(nbs, D_SSM, N_SSM), _tail8(state_ssm_conv[l]),
                                    nbs, PAGE, 1)

        x1, x1n, x1s, x1sn = _merge((y_lru, y_ssm, y_att, gates, xp),
                                    (y_lru_s[::8], y_ssm_s[::PAGE], y_att_s, gates_s, xs), p, l, 256)
        xp, xpn, xs, xsn = _mlp((x1n, x1), (x1sn, x1s), p, l, g_next, min(1024, mp))
        outs[3].append(ks.reshape(nbs, 1, H_ATT, HD))
        outs[4].append(vs.reshape(nbs, 1, H_ATT, HD))
        outs[5].append(lfs[:, :H_ATT].reshape(nbs, 1, H_ATT))
        outs[7].append(h_s.reshape(nbs, D_LRU))
        outs[9].append(jnp.concatenate([state_lru_conv[l][:, 1:], lru_s[:, None, :D_LRU]], axis=1))
        outs[11].append(st_s.reshape(nbs, H_SSM, P_SSM, N_SSM))
        outs[13].append(jnp.concatenate([state_ssm_conv[l][:, 1:], xbc_s[:, None, :]], axis=1))

    st = [jnp.stack(o) if o else None for o in outs]
    kt_all, vt_all, lft_all = kv_prev
    new_k_p = jnp.transpose(kt_all.reshape(DEPTH, nbp, npg, H_ATT, HD, PAGE), (0, 1, 2, 5, 3, 4))
    new_v_p = jnp.transpose(vt_all.reshape(DEPTH, nbp, npg, H_ATT, HD, PAGE), (0, 1, 2, 5, 3, 4))
    new_lf_p = jnp.transpose(lft_all, (0, 1, 2, 4, 3))
    return (xp.reshape(nbp, seq, D_MODEL), xs.reshape(nbs, 1, D_MODEL),
            new_k_p, new_v_p, new_lf_p, st[3], st[4], st[5],
            st[6], st[7], st[8], st[9], st[10], st[11], st[12], st[13])
```

```python
import functools

import jax
import jax.numpy as jnp
from jax import lax
from jax.experimental import pallas as pl
from jax.experimental.pallas import tpu as pltpu

F32 = jnp.float32
BF16 = jnp.bfloat16

D_MODEL = 2048
DEPTH = 4
PAGE = 128
CONV_W = 4
D_LRU = 1024
LRU_BLOCKS = 16
LRU_BW = 64
LRU_C = 8.0
H_SSM = 16
P_SSM = 64
D_SSM = 1024
G_SSM = 2
N_SSM = 128
D_XBC = D_SSM + 2 * G_SSM * N_SSM
H_ATT = 16
HD = 64
D_ATT = 1024
N_BRANCH = 3
D_FF = 4 * D_MODEL
EPS = 1e-6
NEG = -1e30

OFF_LRU = 0
OFF_Z = 2048
OFF_XBC = 3072
OFF_DT = 4608
OFF_QKV = 4624
OFF_F = 7696
OFF_GATES = 7712

VMEM_LIMIT = 56 * 1024 * 1024


def _cparams(sem):
    return pltpu.CompilerParams(dimension_semantics=sem, vmem_limit_bytes=VMEM_LIMIT)


def _dot(a, b):
    return jnp.dot(a, b, preferred_element_type=F32)


def _dot_nt(a, b):
    return lax.dot_general(a, b, (((1,), (1,)), ((), ())), preferred_element_type=F32)


def _split3(x):
    hi = x.astype(BF16)
    r = x - hi.astype(F32)
    mid = r.astype(BF16)
    lo = (r - mid.astype(F32)).astype(BF16)
    return hi, mid, lo


def _dot3_right(x, m01):
    hi, mid, lo = _split3(x)
    return _dot(hi, m01) + _dot(mid, m01) + _dot(lo, m01)


def _dot3_left(m01, x):
    hi, mid, lo = _split3(x)
    return _dot(m01, hi) + _dot(m01, mid) + _dot(m01, lo)


def _sigmoid(x):
    return 0.5 * jnp.tanh(0.5 * x) + 0.5


def _softplus(x):
    return jnp.maximum(x, 0.0) + jnp.log1p(jnp.exp(-jnp.abs(x)))


def _rms(x, g):
    ms = jnp.mean(x * x, axis=-1, keepdims=True)
    return x * lax.rsqrt(ms + EPS) * g


def _rms_kernel(x_ref, g_ref, o_ref):
    o_ref[...] = _rms(x_ref[...], g_ref[...]).astype(o_ref.dtype)


def _rmsnorm(x, g, bm):
    m, d = x.shape
    return pl.pallas_call(
        _rms_kernel,
        out_shape=jax.ShapeDtypeStruct((m, d), BF16),
        grid=(m // bm,),
        in_specs=[pl.BlockSpec((bm, d), lambda i: (i, 0)),
                  pl.BlockSpec((1, d), lambda i: (0, 0))],
        out_specs=pl.BlockSpec((bm, d), lambda i: (i, 0)),
        compiler_params=_cparams(("arbitrary",)),
        name="rmsnorm",
    )(x, g)


def _mm_nt_kernel(x_ref, xs_ref, w_ref, o_ref, os_ref, wb_ref):
    @pl.when(pl.program_id(1) == 0)
    def _():
        wb_ref[...] = w_ref[...].astype(BF16)
        os_ref[...] = _dot_nt(xs_ref[...], wb_ref[...])

    o_ref[...] = _dot_nt(x_ref[...], wb_ref[...])


def _mm_nt(x, xs, w_t, layer, off, n, bn, bm, name):
    m, k = x.shape
    ms = xs.shape[0]
    return pl.pallas_call(
        _mm_nt_kernel,
        out_shape=(jax.ShapeDtypeStruct((m, n), F32), jax.ShapeDtypeStruct((ms, n), F32)),
        grid=(n // bn, m // bm),
        in_specs=[pl.BlockSpec((bm, k), lambda j, i: (i, 0)),
                  pl.BlockSpec((ms, k), lambda j, i: (0, 0)),
                  pl.BlockSpec((None, pl.Element(bn), pl.Element(k)),
                               lambda j, i: (layer, pl.multiple_of(off + j * bn, 8), 0))],
        out_specs=(pl.BlockSpec((bm, bn), lambda j, i: (i, j)),
                   pl.BlockSpec((ms, bn), lambda j, i: (0, j))),
        scratch_shapes=[pltpu.VMEM((bn, k), BF16)],
        compiler_params=_cparams(("arbitrary", "arbitrary")),
        name=name,
    )(x, xs, w_t)


def _mm_small_kernel(x_ref, xs_ref, wa_ref, wb_ref, o_ref, os_ref):
    wa = wa_ref[...].astype(BF16)
    wb = wb_ref[...].astype(BF16)
    x = x_ref[...]
    o_ref[:, 0:128] = _dot_nt(x, wa)
    o_ref[:, 128:256] = _dot_nt(x, wb)

    @pl.when(pl.program_id(0) == 0)
    def _():
        xs = xs_ref[...]
        os_ref[:, 0:128] = _dot_nt(xs, wa)
        os_ref[:, 128:256] = _dot_nt(xs, wb)


def _mm_small(x, xs, w_t, layer, bm):
    m, k = x.shape
    ms = xs.shape[0]
    return pl.pallas_call(
        _mm_small_kernel,
        out_shape=(jax.ShapeDtypeStruct((m, 256), F32), jax.ShapeDtypeStruct((ms, 256), F32)),
        grid=(m // bm,),
        in_specs=[pl.BlockSpec((bm, k), lambda i: (i, 0)),
                  pl.BlockSpec((ms, k), lambda i: (0, 0)),
                  pl.BlockSpec((None, pl.Element(128), pl.Element(k)), lambda i: (layer, OFF_DT, 0)),
                  pl.BlockSpec((None, pl.Element(128), pl.Element(k)), lambda i: (layer, OFF_F, 0))],
        out_specs=(pl.BlockSpec((bm, 256), lambda i: (i, 0)),
                   pl.BlockSpec((ms, 256), lambda i: (0, 0))),
        compiler_params=_cparams(("arbitrary",)),
        name="inproj_small",
    )(x, xs, w_t, w_t)


def _causal_conv(ext_ref, x, cw, cb, rows):
    ext_ref[8:8 + rows, :] = x
    out = cb + ext_ref[5:5 + rows, :] * cw[0:1]
    out = out + ext_ref[6:6 + rows, :] * cw[1:2]
    out = out + ext_ref[7:7 + rows, :] * cw[2:3]
    out = out + x * cw[3:4]
    ext_ref[0:8, :] = ext_ref[rows:rows + 8, :]
    return out


def _lru_kernel(u_ref, g_ref, cw_ref, cb_ref, wa_ref, wx_ref, ba_ref, bx_ref, lam_ref,
                h0_ref, tail0_ref, y_ref, hl_ref, ext_ref, hc_ref,
                *, tt, last_row):
    t = pl.program_id(1)
    nt = pl.num_programs(1)

    @pl.when(t == 0)
    def _():
        ext_ref[0:8, :] = tail0_ref[...]
        hc_ref[...] = h0_ref[...]

    uc = _causal_conv(ext_ref, u_ref[...], cw_ref[...], cb_ref[...], tt)
    ucb = uc.astype(BF16)
    ra = jnp.concatenate([_dot(ucb[:, 256 * c:256 * (c + 1)], wa_ref[c]) for c in range(4)], axis=1)
    rx = jnp.concatenate([_dot(ucb[:, 256 * c:256 * (c + 1)], wx_ref[c]) for c in range(4)], axis=1)
    r = _sigmoid(ra + ba_ref[...])
    gi = _sigmoid(rx + bx_ref[...])
    log_a = (-LRU_C * _softplus(-lam_ref[...])) * r
    a = jnp.exp(log_a)
    th = jnp.tanh(log_a)
    one_m_a2 = -2.0 * th / (1.0 - th)
    root = jnp.where(one_m_a2 > 0.0, one_m_a2 * lax.rsqrt(one_m_a2), 0.0)
    bvec = root * (gi * uc)

    row8 = lax.broadcasted_iota(jnp.int32, (tt, D_LRU), 0) & 7
    for k in (1, 2, 4):
        inside = row8 >= k
        a_prev = jnp.where(inside, pltpu.roll(a, k, 0), 1.0)
        b_prev = jnp.where(inside, pltpu.roll(bvec, k, 0), 0.0)
        bvec = bvec + a * b_prev
        a = a * a_prev
    h_in = hc_ref[...]
    groups = []
    for r in range(tt // 8):
        hg = bvec[8 * r:8 * (r + 1)] + a[8 * r:8 * (r + 1)] * h_in
        groups.append(hg)
        h_in = hg[7:8]
    hc_ref[...] = h_in
    h = jnp.concatenate(groups, axis=0)
    g = g_ref[...]
    gelu = 0.5 * g * (1.0 + jnp.tanh(0.7978845608028654 * (g + 0.044715 * (g * g * g))))
    y_ref[...] = (h * gelu).astype(y_ref.dtype)

    @pl.when(t == nt - 1)
    def _():
        hl_ref[...] = h[last_row:last_row + 1]


def _lru_branch(lru, p, h0, tail0, nb, seq, tt, last_row):
    nt = seq // tt
    kern = functools.partial(_lru_kernel, tt=tt, last_row=last_row)
    vec = lambda: pl.BlockSpec((1, D_LRU), lambda b, t: (0, 0))
    return pl.pallas_call(
        kern,
        out_shape=(jax.ShapeDtypeStruct((nb * seq, D_LRU), BF16),
                   jax.ShapeDtypeStruct((nb, 1, D_LRU), F32)),
        grid=(nb, nt),
        in_specs=[pl.BlockSpec((tt, D_LRU), lambda b, t: (b * nt + t, 0)),
                  pl.BlockSpec((tt, D_LRU), lambda b, t: (b * nt + t, 1)),
                  pl.BlockSpec((CONV_W, D_LRU), lambda b, t: (0, 0)),
                  vec(),
                  pl.BlockSpec((4, 256, 256), lambda b, t: (0, 0, 0)),
                  pl.BlockSpec((4, 256, 256), lambda b, t: (0, 0, 0)),
                  vec(), vec(), vec(),
                  pl.BlockSpec((None, 1, D_LRU), lambda b, t: (b, 0, 0)),
                  pl.BlockSpec((None, 8, D_LRU), lambda b, t: (b, 0, 0))],
        out_specs=(pl.BlockSpec((tt, D_LRU), lambda b, t: (b * nt + t, 0)),
                   pl.BlockSpec((None, 1, D_LRU), lambda b, t: (b, 0, 0))),
        scratch_shapes=[pltpu.VMEM((tt + 8, D_LRU), F32),
                        pltpu.VMEM((1, D_LRU), F32)],
        compiler_params=_cparams(("arbitrary", "arbitrary")),
        name="rglru",
    )(lru, lru, p["lru_conv_w"], p["lru_conv_b"], p["wa_bd"], p["wx_bd"],
      p["lru_ba"], p["lru_bx"], p["lru_lambda"], h0, tail0)


def _ssd_kernel(*refs, n_valid, ride):
    if ride:
        refs = refs[1:]
        _decode_scores_pages(refs[13], refs[14:14 + ride], refs[14 + ride:14 + 2 * ride],
                             refs[16 + 2 * ride], refs[17 + 2 * ride])
        refs = refs[:13] + refs[14 + 2 * ride:16 + 2 * ride] + refs[18 + 2 * ride:]
    (xbc_ref, z_ref, dtf_ref, cw_ref, cb_ref, dtb_ref, alog_ref, dx_ref, ng_ref,
     tri_ref, e_ref, st0_ref, tail0_ref, y_ref, stf_ref, ext_ref, st_ref) = refs
    c = pl.program_id(1)
    nc = pl.num_programs(1)
    q = PAGE

    @pl.when(c == 0)
    def _():
        ext_ref[0:8, :] = tail0_ref[...]
        st_ref[...] = st0_ref[...].T

    conv = _causal_conv(ext_ref, xbc_ref[...], cw_ref[...], cb_ref[...], q)
    xc = conv * _sigmoid(conv)
    xs = xc[:, 0:D_SSM]
    bm = xc[:, D_SSM:D_SSM + 256]
    cm = xc[:, D_SSM + 256:D_SSM + 512]

    row = lax.broadcasted_iota(jnp.int32, (q, q), 0)
    col = lax.broadcasted_iota(jnp.int32, (q, q), 1)
    dt = _softplus(dtf_ref[:, 0:128] + dtb_ref[...])
    if n_valid < q:
        dt = jnp.where(row < n_valid, dt, 0.0)
    a_neg = -jnp.exp(alog_ref[...])
    dta = dt * a_neg
    cs = _dot3_left(tri_ref[...], dta)
    cs_t = cs.T
    dt_t = dt.T
    cs_last = cs[q - 1:q, :]
    e01 = e_ref[...]
    exp_cs_x = _dot3_right(jnp.exp(cs), e01)
    ws_x = _dot3_right(jnp.exp(cs_last - cs) * dt, e01)
    dec_x = _dot3_right(jnp.broadcast_to(jnp.exp(cs_last), (8, 128)), e01)[0:1]
    dx = dx_ref[...]
    lane_lo = col < 64
    tril = col <= row

    ys = []
    for g in range(G_SSM):
        bg = bm[:, 128 * g:128 * (g + 1)]
        cgb = cm[:, 128 * g:128 * (g + 1)].astype(BF16)
        gl = slice(512 * g, 512 * (g + 1))
        cb_mat = _dot_nt(cgb, bg.astype(BF16))
        st_g = st_ref[:, gl]
        yoff = _dot(cgb, st_g.astype(BF16)) * exp_cs_x[:, gl]
        xw = (xs[:, gl] * ws_x[:, gl]).astype(BF16)
        st_ref[:, gl] = st_g * dec_x[:, gl] + _dot(bg.T.astype(BF16), xw)
        for jj in range(4):
            ws = []
            for a in range(2):
                h = 8 * g + 2 * jj + a
                seg = cs[:, h:h + 1] - cs_t[h:h + 1, :]
                lm = jnp.exp(jnp.where(tril, seg, NEG))
                ws.append((cb_mat * lm * dt_t[h:h + 1, :]).astype(BF16))
            wcat = jnp.concatenate(ws, axis=1)
            ll = slice(512 * g + 128 * jj, 512 * g + 128 * (jj + 1))
            x2 = xs[:, ll]
            xst = jnp.concatenate([jnp.where(lane_lo, x2, 0.0), jnp.where(lane_lo, 0.0, x2)],
                                  axis=0).astype(BF16)
            ys.append(_dot(wcat, xst) + yoff[:, 128 * jj:128 * (jj + 1)] + dx[:, ll] * x2)
    y = jnp.concatenate(ys, axis=1)
    z = z_ref[...]
    y_ref[...] = _rms(y * (z * _sigmoid(z)), ng_ref[...]).astype(y_ref.dtype)

    @pl.when(c == nc - 1)
    def _():
        stf_ref[...] = st_ref[...].T


def _rider_pages(page_table, n_steps):
    nbs, npg = page_table.shape
    steps_per_row = n_steps // nbs
    assert steps_per_row * nbs == n_steps and npg % steps_per_row == 0
    return steps_per_row, npg // steps_per_row


def _ssd_branch(xbc, z, dtf, p, st0, tail0, nb, seq, n_valid, decode=None):
    nc = seq // PAGE
    const = lambda shape: pl.BlockSpec(shape, lambda b, c, *_: (0,) * len(shape))
    in_specs = [pl.BlockSpec((PAGE, D_XBC), lambda b, c, *_: (b * nc + c, 0)),
                pl.BlockSpec((PAGE, D_SSM), lambda b, c, *_: (b * nc + c, 0)),
                pl.BlockSpec((PAGE, 256), lambda b, c, *_: (b * nc + c, 0)),
                const((CONV_W, D_XBC)), const((1, D_XBC)),
                const((1, 128)), const((1, 128)), const((1, D_SSM)), const((1, D_SSM)),
                const((128, 128)), const((128, D_SSM)),
                pl.BlockSpec((None, D_SSM, N_SSM), lambda b, c, *_: (b, 0, 0)),
                pl.BlockSpec((None, 8, D_XBC), lambda b, c, *_: (b, 0, 0))]
    out_specs = [pl.BlockSpec((PAGE, D_SSM), lambda b, c, *_: (b * nc + c, 0)),
                 pl.BlockSpec((None, D_SSM, N_SSM), lambda b, c, *_: (b, 0, 0))]
    out_shape = [jax.ShapeDtypeStruct((nb * seq, D_SSM), BF16),
                 jax.ShapeDtypeStruct((nb, D_SSM, N_SSM), F32)]
    args = [xbc, z, dtf, p["ssm_conv_w"], p["ssm_conv_b"], p["dt_bias128"], p["a_log128"],
            p["d_x"], p["ssm_norm_g"], p["tri_incl"], p["expand"], st0, tail0]
    scratch = [pltpu.VMEM((PAGE + 8, D_XBC), F32), pltpu.VMEM((N_SSM, D_SSM), F32)]
    ride = 0
    if decode is not None:
        page_table, qcol, cache_kt, cache_lft, layer = decode
        nbs, npg = page_table.shape
        spr, ride = _rider_pages(page_table, nb * nc)
        row = lambda b, c: (b * nc + c) // spr
        grp = lambda b, c: (b * nc + c) % spr
        pg = lambda g: (lambda b, c, pt: (layer, pt[row(b, c), grp(b, c) * ride + g], 0, 0))
        in_specs += [pl.BlockSpec((None, D_ATT, PAGE), lambda b, c, pt: (row(b, c), 0, 0))]
        in_specs += [pl.BlockSpec((None, None, D_ATT, PAGE), pg(g)) for g in range(ride)]
        in_specs += [pl.BlockSpec((None, None, H_ATT, PAGE), pg(g)) for g in range(ride)]
        out_specs += [pl.BlockSpec((None, ride, H_ATT, PAGE), lambda b, c, pt: (row(b, c), grp(b, c), 0, 0))] * 2
        out_shape += [jax.ShapeDtypeStruct((nbs, npg, H_ATT, PAGE), F32)] * 2
        args = [page_table] + args + [qcol] + [cache_kt] * ride + [cache_lft] * ride
    grid_spec = pltpu.PrefetchScalarGridSpec(
        num_scalar_prefetch=1 if decode is not None else 0,
        grid=(nb, nc), in_specs=in_specs, out_specs=tuple(out_specs), scratch_shapes=scratch)
    return pl.pallas_call(
        functools.partial(_ssd_kernel, n_valid=n_valid, ride=ride),
        out_shape=tuple(out_shape),
        grid_spec=grid_spec,
        compiler_params=_cparams(("arbitrary", "arbitrary")),
        name="ssd",
    )(*args)


def _head_rms(x, bd, g):
    x2 = x * x
    hi = x2.astype(BF16)
    lo = (x2 - hi.astype(F32)).astype(BF16)
    ss = jnp.concatenate(
        [_dot(hi[:, 256 * c:256 * (c + 1)], bd) + _dot(lo[:, 256 * c:256 * (c + 1)], bd)
         for c in range(4)], axis=1)
    return x * lax.rsqrt(ss * (1.0 / HD) + EPS) * g


def _log_sigmoid(x):
    return -_softplus(-x)


AUG = 16
LOG2E = 1.4426950408889634


def _prep_kernel(*refs, aliased):
    (qkv_ref, dtf_ref, gq_ref, gk_ref, fb_ref, bd_ref, triu_ref, ones_ref, sel_ref, augc_ref) = refs[:10]
    refs = refs[10 + (3 if aliased else 0):]
    q_ref, kta_ref, vta_ref, kt_ref, vt_ref, lft_ref, car_ref = refs

    @pl.when(pl.program_id(1) == 0)
    def _():
        car_ref[...] = jnp.zeros_like(car_ref)

    bd = bd_ref[...]
    q = qkv_ref[:, 0:D_ATT]
    k = qkv_ref[:, D_ATT:2 * D_ATT]
    v = qkv_ref[:, 2 * D_ATT:3 * D_ATT]
    kt = _head_rms(k, bd, gk_ref[...]).T
    kt_ref[...] = kt
    vt = v.T
    vt_ref[...] = vt

    lf_t = _log_sigmoid(dtf_ref[:, 128:256] + fb_ref[...]).T
    lft_ref[...] = lf_t[0:H_ATT]
    f_t = _dot3_right(lf_t, triu_ref[...]) + car_ref[...]
    car_ref[...] = car_ref[...] + _dot3_right(lf_t, ones_ref[...])
    fs_t = f_t * LOG2E
    hi_t, mid_t, lo_t = _split3(fs_t)

    qn = _head_rms(q, bd, gq_ref[...]) * (HD ** -0.5 * LOG2E)
    lane_lo = lax.broadcasted_iota(jnp.int32, (PAGE, 128), 1) < 64
    blocks = []
    for m in range(H_ATT // 2):
        x2 = qn[:, 128 * m:128 * (m + 1)]
        blocks.append(jnp.where(lane_lo, x2, 0.0))
        blocks.append(jnp.where(lane_lo, pltpu.roll(x2, 64, 1), 0.0))
    parts = jnp.concatenate(_split3(fs_t.T), axis=1)
    q_ref[...] = (jnp.concatenate(blocks, axis=1) + _dot(parts, sel_ref[...]) + augc_ref[...]).astype(BF16)

    row = lax.broadcasted_iota(jnp.int32, (AUG, PAGE), 0)
    zpad = jnp.zeros((128 - HD - AUG, PAGE), BF16)
    v_aug = jnp.where(row == 0, 1.0, 0.0).astype(BF16)
    hi_f, mid_f, lo_f = hi_t.astype(F32), mid_t.astype(F32), lo_t.astype(F32)
    for h in range(H_ATT):
        k_aug = jnp.where(row < 3, 1.0,
                          jnp.where(row == 3, -hi_f[h:h + 1],
                                    jnp.where(row == 4, -mid_f[h:h + 1],
                                              jnp.where(row == 5, -lo_f[h:h + 1], 0.0)))).astype(BF16)
        kta_ref[h] = jnp.concatenate([kt[HD * h:HD * (h + 1)].astype(BF16), k_aug, zpad], axis=0)
        vta_ref[h] = jnp.concatenate([vt[HD * h:HD * (h + 1)].astype(BF16), v_aug, zpad], axis=0)


def _prep(qkv, dtf, p, nb, seq, layer, prev):
    npg = seq // PAGE
    m = nb * seq
    const = lambda shape: pl.BlockSpec(shape, lambda b, c: (0,) * len(shape))
    headpage = pl.BlockSpec((None, None, H_ATT, 128, PAGE), lambda b, c: (b, c, 0, 0, 0))
    stacked = lambda rows: pl.BlockSpec((None, None, None, rows, PAGE), lambda b, c: (layer, b, c, 0, 0))
    args = [qkv, dtf, p["gq_x"], p["gk_x"], p["f_bias128"], p["bd_ones"], p["tri_incl_t"], p["ones128"],
            p["sel_big"], p["aug_const"]]
    in_specs = [pl.BlockSpec((PAGE, 3 * D_ATT), lambda b, c: (b * npg + c, 0)),
                pl.BlockSpec((PAGE, 256), lambda b, c: (b * npg + c, 0)),
                const((1, D_ATT)), const((1, D_ATT)), const((1, 128)),
                const((256, 256)), const((128, 128)), const((128, 128)),
                const((3 * 128, H_ATT * 128)), const((1, H_ATT * 128))]
    aliases = {}
    if prev is not None:
        args += list(prev)
        in_specs += [pl.BlockSpec(memory_space=pl.ANY)] * 3
        aliases = {10: 3, 11: 4, 12: 5}
    return pl.pallas_call(
        functools.partial(_prep_kernel, aliased=prev is not None),
        out_shape=(jax.ShapeDtypeStruct((m, H_ATT * 128), BF16),
                   jax.ShapeDtypeStruct((nb, npg, H_ATT, 128, PAGE), BF16),
                   jax.ShapeDtypeStruct((nb, npg, H_ATT, 128, PAGE), BF16),
                   jax.ShapeDtypeStruct((DEPTH, nb, npg, D_ATT, PAGE), F32),
                   jax.ShapeDtypeStruct((DEPTH, nb, npg, D_ATT, PAGE), F32),
                   jax.ShapeDtypeStruct((DEPTH, nb, npg, H_ATT, PAGE), F32)),
        grid=(nb, npg),
        in_specs=in_specs,
        out_specs=(pl.BlockSpec((PAGE, H_ATT * 128), lambda b, c: (b * npg + c, 0)),
                   headpage, headpage, stacked(D_ATT), stacked(D_ATT), stacked(H_ATT)),
        scratch_shapes=[pltpu.VMEM((128, 128), F32)],
        input_output_aliases=aliases,
        compiler_params=_cparams(("arbitrary", "arbitrary")),
        name="attn_prep",
    )(*args)


def _attn_kernel(*refs, tq, pages, ride, steps_per_row):
    if ride:
        q_ref, kt_ref, vt_ref, p_ref, pn_ref, vn_ref = refs[1:7]
        v_refs = refs[7:7 + ride]
        o_ref, od_ref, acc_ref = refs[7 + ride:]
        step = ((pl.program_id(0) * pl.num_programs(1) + pl.program_id(1)) * pl.num_programs(2)
                + pl.program_id(2))
        grp = step % steps_per_row
        _decode_values_pages(p_ref, pn_ref, vn_ref, v_refs, od_ref, acc_ref,
                             grp == 0, grp == steps_per_row - 1)
    else:
        q_ref, kt_ref, vt_ref, o_ref = refs
    i = pl.program_id(2)
    tk = pages * PAGE
    assert tq == tk
    diag = (lax.broadcasted_iota(jnp.int32, (tq, tk), 1) <= lax.broadcasted_iota(jnp.int32, (tq, tk), 0))
    qs = [q_ref[:, 0:128], q_ref[:, 128:256]]

    def tile(page0, n_pages, carry, masked):
        out = []
        for a in range(2):
            m_old, acc = carry[a]
            kt = jnp.concatenate([kt_ref[page0 + pg, a] for pg in range(n_pages)], axis=1)
            vt = jnp.concatenate([vt_ref[page0 + pg, a] for pg in range(n_pages)], axis=1)
            s = _dot(qs[a], kt)
            if masked:
                s = jnp.where(diag, s, NEG)
            m_new = jnp.maximum(m_old, jnp.max(s, axis=-1, keepdims=True))
            pr = jnp.exp2(s - m_new).astype(BF16)
            acc = acc * jnp.exp2(m_old - m_new) + _dot_nt(pr, vt)
            out.append((m_new, acc))
        return tuple(out)

    init = tuple((jnp.full((tq, 1), NEG, F32), jnp.zeros((tq, 128), F32)) for _ in range(2))
    carry = lax.fori_loop(0, i // 2, lambda kk, c: tile(kk * 2 * pages, 2 * pages, c, False), init)
    carry = lax.cond(i % 2 == 1, lambda c: tile((i - 1) * pages, pages, c, False), lambda c: c, carry)
    (_, acc0), (_, acc1) = tile(i * pages, pages, carry, True)
    o0 = acc0 * (1.0 / acc0[:, HD:HD + 1])
    o1 = acc1 * (1.0 / acc1[:, HD:HD + 1])
    lane_lo = lax.broadcasted_iota(jnp.int32, (tq, 128), 1) < HD
    o_ref[...] = jnp.where(lane_lo, o0, pltpu.roll(o1, HD, 1)).astype(o_ref.dtype)


def _attention(q_aug, kta, vta, nb, seq, tq=512, decode=None):
    nq = seq // tq
    npg = seq // PAGE
    nj = H_ATT // 2
    in_specs = [pl.BlockSpec((tq, 256), lambda b, j, i, *_: (b * nq + i, j)),
                pl.BlockSpec((None, npg, 2, 128, PAGE), lambda b, j, i, *_: (b, 0, j, 0, 0)),
                pl.BlockSpec((None, npg, 2, 128, PAGE), lambda b, j, i, *_: (b, 0, j, 0, 0))]
    out_specs = [pl.BlockSpec((tq, 128), lambda b, j, i, *_: (b * nq + i, j))]
    out_shape = [jax.ShapeDtypeStruct((nb * seq, D_ATT), BF16)]
    args = [q_aug, kta, vta]
    scratch = []
    ride, spr = 0, 1
    if decode is not None:
        page_table, pmat, pnew, vncol, cache_vt, layer = decode
        nbs = page_table.shape[0]
        spr, ride = _rider_pages(page_table, nb * nj * nq)
        step = lambda b, j, i: (b * nj + j) * nq + i
        row = lambda b, j, i: step(b, j, i) // spr
        grp = lambda b, j, i: step(b, j, i) % spr
        pg = lambda g: (lambda b, j, i, pt: (layer, pt[row(b, j, i), grp(b, j, i) * ride + g], 0, 0))
        in_specs += [pl.BlockSpec((None, ride, H_ATT, PAGE), lambda b, j, i, pt: (row(b, j, i), grp(b, j, i), 0, 0)),
                     pl.BlockSpec((None, H_ATT, PAGE), lambda b, j, i, pt: (row(b, j, i), 0, 0)),
                     pl.BlockSpec((None, D_ATT, PAGE), lambda b, j, i, pt: (row(b, j, i), 0, 0))]
        in_specs += [pl.BlockSpec((None, None, D_ATT, PAGE), pg(g)) for g in range(ride)]
        out_specs += [pl.BlockSpec((None, D_ATT, PAGE), lambda b, j, i, pt: (row(b, j, i), 0, 0))]
        out_shape += [jax.ShapeDtypeStruct((nbs, D_ATT, PAGE), F32)]
        args = [page_table] + args + [pmat, pnew, vncol] + [cache_vt] * ride
        scratch = [pltpu.VMEM((D_ATT, PAGE), F32)]
    grid_spec = pltpu.PrefetchScalarGridSpec(
        num_scalar_prefetch=1 if decode is not None else 0,
        grid=(nb, nj, nq), in_specs=in_specs, out_specs=tuple(out_specs), scratch_shapes=scratch)
    out = pl.pallas_call(
        functools.partial(_attn_kernel, tq=tq, pages=tq // PAGE, ride=ride, steps_per_row=spr),
        out_shape=tuple(out_shape),
        grid_spec=grid_spec,
        compiler_params=_cparams(("arbitrary", "arbitrary", "arbitrary")),
        name="fox_attention",
    )(*args)
    return out if decode is not None else out[0]


def _merge_kernel(bg_ref, wb_ref, wo_ref, g2_ref,
                  y0_ref, y1_ref, y2_ref, gates_ref, x_ref,
                  sy0_ref, sy1_ref, sy2_ref, sgates_ref, sx_ref,
                  x1_ref, xn_ref, sx1_ref, sxn_ref):
    def rows(ys, gates, x, x1_out, xn_out):
        merged = None
        for n in range(N_BRANCH):
            pb = _dot(ys[n][...], wb_ref[n])
            gate = _sigmoid(gates[:, D_MODEL * n:D_MODEL * (n + 1)] + bg_ref[n:n + 1, :])
            merged = gate * pb if merged is None else merged + gate * pb
        x1 = x[...] + _dot(merged.astype(BF16), wo_ref[...])
        x1_out[...] = x1
        xn_out[...] = _rms(x1, g2_ref[...]).astype(xn_out.dtype)

    rows((y0_ref, y1_ref, y2_ref), gates_ref, x_ref, x1_ref, xn_ref)

    @pl.when(pl.program_id(0) == 0)
    def _():
        rows((sy0_ref, sy1_ref, sy2_ref), sgates_ref, sx_ref, sx1_ref, sxn_ref)


def _merge(prompt, sample, p, layer, bm):
    m = prompt[4].shape[0]
    ms = sample[4].shape[0]
    row = lambda w: pl.BlockSpec((bm, w), lambda i: (i, 0))
    srow = lambda w: pl.BlockSpec((ms, w), lambda i: (0, 0))
    widths = (D_LRU, D_SSM, D_ATT, N_BRANCH * D_MODEL, D_MODEL)
    return pl.pallas_call(
        _merge_kernel,
        out_shape=(jax.ShapeDtypeStruct((m, D_MODEL), F32),
                   jax.ShapeDtypeStruct((m, D_MODEL), BF16),
                   jax.ShapeDtypeStruct((ms, D_MODEL), F32),
                   jax.ShapeDtypeStruct((ms, D_MODEL), BF16)),
        grid=(m // bm,),
        in_specs=[pl.BlockSpec((N_BRANCH, D_MODEL), lambda i: (0, 0)),
                  pl.BlockSpec((None, N_BRANCH, D_LRU, D_MODEL), lambda i: (layer, 0, 0, 0),
                               pipeline_mode=pl.Buffered(1)),
                  pl.BlockSpec((None, D_MODEL, D_MODEL), lambda i: (layer, 0, 0),
                               pipeline_mode=pl.Buffered(1)),
                  pl.BlockSpec((1, D_MODEL), lambda i: (0, 0))]
                 + [row(w) for w in widths] + [srow(w) for w in widths],
        out_specs=(row(D_MODEL), row(D_MODEL), srow(D_MODEL), srow(D_MODEL)),
        compiler_params=_cparams(("arbitrary",)),
        name="merge_out",
    )(p["b_gate"], p["w_branch_bf"], p["w_out_bf"], p["norm2_g"], *prompt, *sample)


def _mlp_kernel(wu_ref, wd_ref, gn_ref, xn_ref, x1_ref, sxn_ref, sx1_ref, o_ref, on_ref, so_ref, son_ref):
    i = pl.program_id(0)
    f = pl.program_id(1)
    nf = pl.num_programs(1)

    def rows(xn, x1, out, out_n):
        @pl.when(f == 0)
        def _():
            out[...] = x1[...]

        h = jnp.maximum(_dot(xn[...], wu_ref[...]), 0.0)
        out[...] += _dot((h * h).astype(BF16), wd_ref[...])

        @pl.when(f == nf - 1)
        def _():
            out_n[...] = _rms(out[...], gn_ref[...]).astype(out_n.dtype)

    rows(xn_ref, x1_ref, o_ref, on_ref)

    @pl.when(i == 0)
    def _():
        rows(sxn_ref, sx1_ref, so_ref, son_ref)


def _mlp(prompt, sample, p, layer, g_next, bm, bf=512):
    m = prompt[1].shape[0]
    ms = sample[1].shape[0]
    once = pl.Buffered(1)
    row = lambda: pl.BlockSpec((bm, D_MODEL), lambda i, f: (i, 0), pipeline_mode=once)
    srow = lambda: pl.BlockSpec((ms, D_MODEL), lambda i, f: (0, 0))
    return pl.pallas_call(
        _mlp_kernel,
        out_shape=(jax.ShapeDtypeStruct((m, D_MODEL), F32),
                   jax.ShapeDtypeStruct((m, D_MODEL), BF16),
                   jax.ShapeDtypeStruct((ms, D_MODEL), F32),
                   jax.ShapeDtypeStruct((ms, D_MODEL), BF16)),
        grid=(m // bm, D_FF // bf),
        in_specs=[pl.BlockSpec((None, D_MODEL, bf), lambda i, f: (layer, 0, f)),
                  pl.BlockSpec((None, bf, D_MODEL), lambda i, f: (layer, f, 0)),
                  pl.BlockSpec((1, D_MODEL), lambda i, f: (0, 0)),
                  row(), row(), srow(), srow()],
        out_specs=(row(), row(), srow(), srow()),
        compiler_params=_cparams(("arbitrary", "arbitrary")),
        name="mlp",
    )(p["w_up_bf"], p["w_down_bf"], g_next, *prompt, *sample)


def _prep_s_kernel(qkv_ref, dtf_ref, gq_ref, gk_ref, fb_ref, bd_ref, q_ref, k_ref, lf_ref):
    bd = bd_ref[...]
    q_ref[...] = _head_rms(qkv_ref[:, 0:D_ATT], bd, gq_ref[...]) * (HD ** -0.5)
    k_ref[...] = _head_rms(qkv_ref[:, D_ATT:2 * D_ATT], bd, gk_ref[...])
    lf_ref[...] = _log_sigmoid(dtf_ref[:, 128:256] + fb_ref[...])


def _prep_s(qkv, dtf, p):
    nb = qkv.shape[0]
    full = lambda a: pl.BlockSpec(a.shape, lambda: (0,) * a.ndim)
    args = (qkv, dtf, p["gq_x"], p["gk_x"], p["f_bias128"], p["bd_ones"])
    return pl.pallas_call(
        _prep_s_kernel,
        out_shape=(jax.ShapeDtypeStruct((nb, D_ATT), F32),
                   jax.ShapeDtypeStruct((nb, D_ATT), F32),
                   jax.ShapeDtypeStruct((nb, 128), F32)),
        in_specs=[full(a) for a in args],
        out_specs=(pl.BlockSpec((nb, D_ATT), lambda: (0, 0)),
                   pl.BlockSpec((nb, D_ATT), lambda: (0, 0)),
                   pl.BlockSpec((nb, 128), lambda: (0, 0))),
        name="attn_prep_sample",
    )(*args)


def _decode_scores_pages(qc_ref, k_refs, lf_refs, s_ref, lfo_ref):
    qc = qc_ref[...]
    for g in range(len(k_refs)):
        prod = k_refs[g][...] * qc
        s_ref[g] = jnp.sum(prod.reshape(H_ATT, HD, PAGE), axis=1)
        lfo_ref[g] = lf_refs[g][...]


def _decode_values_pages(p_ref, pn_ref, vn_ref, v_refs, o_ref, acc_ref, first, last):
    @pl.when(first)
    def _():
        acc_ref[...] = jnp.zeros_like(acc_ref)

    for h in range(H_ATT):
        rows = slice(HD * h, HD * (h + 1))
        acc = acc_ref[rows, :]
        for g in range(len(v_refs)):
            acc = acc + v_refs[g][rows, :] * p_ref[g, h:h + 1, :]
        acc_ref[rows, :] = acc

    @pl.when(last)
    def _():
        pnb = jnp.broadcast_to(pn_ref[...][:, None, :], (H_ATT, HD, PAGE)).reshape(D_ATT, PAGE)
        o_ref[...] = jnp.sum(acc_ref[...], axis=-1, keepdims=True) + pnb * vn_ref[...]


def _dec_softmax_kernel(s_ref, lf_ref, qc_ref, kn_ref, lfn_ref, triu_ref, ones_ref,
                        p_ref, pn_ref, r_ref):
    npg = s_ref.shape[0]
    lf = lf_ref[...].reshape(npg * H_ATT, PAGE)
    r_ref[...] = _dot3_right(lf, triu_ref[...]).reshape(npg, H_ATT, PAGE)
    p_ref[...] = _dot3_right(lf, ones_ref[...]).reshape(npg, H_ATT, PAGE)

    def body(t, carry):
        pg = npg - 1 - t
        r_ref[pg] = r_ref[pg] + carry
        return carry + p_ref[pg]

    lax.fori_loop(0, npg, body, lfn_ref[...])
    s = s_ref[...] + r_ref[...]
    s_new = jnp.sum((qc_ref[...] * kn_ref[...]).reshape(H_ATT, HD, PAGE), axis=1)
    m = jnp.max(jnp.max(s, axis=0), axis=-1, keepdims=True)
    m = jnp.maximum(m, s_new)
    e = jnp.exp(s - m[None])
    e_new = jnp.exp(s_new - m)
    denom = jnp.sum(jnp.sum(e, axis=0), axis=-1, keepdims=True) + e_new
    inv = 1.0 / denom
    p_ref[...] = e * inv[None]
    pn_ref[...] = e_new * inv


def _dec_softmax(s, lfg, qcol, kncol, lfncol, p):
    nb, npg = s.shape[:2]
    blk4 = pl.BlockSpec((None, npg, H_ATT, PAGE), lambda b: (b, 0, 0, 0))
    col = pl.BlockSpec((None, D_ATT, PAGE), lambda b: (b, 0, 0))
    hrow = pl.BlockSpec((None, H_ATT, PAGE), lambda b: (b, 0, 0))
    c128 = pl.BlockSpec((128, 128), lambda b: (0, 0))
    return pl.pallas_call(
        _dec_softmax_kernel,
        out_shape=(jax.ShapeDtypeStruct((nb, npg, H_ATT, PAGE), F32),
                   jax.ShapeDtypeStruct((nb, H_ATT, PAGE), F32)),
        grid=(nb,),
        in_specs=[blk4, blk4, col, col, hrow, c128, c128],
        out_specs=(blk4, hrow),
        scratch_shapes=[pltpu.VMEM((npg, H_ATT, PAGE), F32)],
        compiler_params=_cparams(("arbitrary",)),
        name="decode_softmax",
    )(s, lfg, qcol, kncol, lfncol, p["tri_strict_t"], p["ones128"])


def _block_diag_256(w):
    w4 = w.reshape(4, 4, LRU_BW, LRU_BW)
    eye = jnp.eye(4, dtype=w.dtype)
    bd = jnp.einsum("cjab,jk->cjakb", w4, eye)
    return bd.reshape(4, 256, 256).astype(BF16)


def _pad_lanes(v, width=128):
    return jnp.zeros((1, width), F32).at[0, :v.shape[0]].set(v)


def _constants():
    r = jnp.arange(128)
    tri_incl = (r[None, :] <= r[:, None]).astype(BF16)
    src = jnp.arange(3 * 128)
    dst = jnp.arange(H_ATT * 128)
    sel_big = ((dst[None, :] == 128 * (src[:, None] % 128) + HD + src[:, None] // 128)
               & (src[:, None] % 128 < H_ATT)).astype(BF16)
    aug_const = ((dst % 128 >= HD + 3) & (dst % 128 < HD + 6)).astype(F32)[None]
    return {
        "sel_big": sel_big,
        "aug_const": aug_const,
        "tri_incl": tri_incl,
        "tri_incl_t": tri_incl.T,
        "tri_strict_t": (r[:, None] > r[None, :]).astype(BF16),
        "ones128": jnp.ones((128, 128), BF16),
        "expand": (jnp.arange(D_SSM)[None, :] // P_SSM == r[:, None]).astype(BF16),
        "bd_ones": (jnp.arange(256)[:, None] // HD == jnp.arange(256)[None, :] // HD).astype(BF16),
    }


def _layer_params(l, a, consts):
    p = dict(consts)
    p["lru_conv_w"] = a["lru_conv_w"][l]
    p["lru_conv_b"] = a["lru_conv_b"][l][None]
    p["wa_bd"] = _block_diag_256(a["lru_wa"][l])
    p["wx_bd"] = _block_diag_256(a["lru_wx"][l])
    p["lru_ba"] = a["lru_ba"][l][None]
    p["lru_bx"] = a["lru_bx"][l][None]
    p["lru_lambda"] = a["lru_lambda"][l][None]
    p["ssm_conv_w"] = a["ssm_conv_w"][l]
    p["ssm_conv_b"] = a["ssm_conv_b"][l][None]
    p["dt_bias128"] = _pad_lanes(a["ssm_dt_bias"][l])
    p["a_log128"] = _pad_lanes(a["ssm_a_log"][l])
    p["d_x"] = jnp.repeat(a["ssm_d"][l], P_SSM)[None]
    p["ssm_norm_g"] = a["ssm_norm_g"][l][None]
    p["gq_x"] = jnp.tile(a["att_q_norm_g"][l], H_ATT)[None]
    p["gk_x"] = jnp.tile(a["att_k_norm_g"][l], H_ATT)[None]
    p["f_bias128"] = _pad_lanes(a["att_f_bias"][l])
    p["b_gate"] = a["b_gate"][l]
    p["norm2_g"] = a["norm2_g"][l][None]
    for name in ("w_branch_bf", "w_out_bf", "w_up_bf", "w_down_bf"):
        p[name] = a[name]
    return p


def _in_proj(xn, xsn, w_t, layer, bm):
    lru = _mm_nt(xn, xsn, w_t, layer, OFF_LRU, 2048, 1024, bm, "inproj_lru")
    z = _mm_nt(xn, xsn, w_t, layer, OFF_Z, 1024, 1024, bm, "inproj_z")
    xbc = _mm_nt(xn, xsn, w_t, layer, OFF_XBC, 1536, 768, bm, "inproj_xbc")
    qkv = _mm_nt(xn, xsn, w_t, layer, OFF_QKV, 3072, 1024, bm, "inproj_qkv")
    gates = _mm_nt(xn, xsn, w_t, layer, OFF_GATES, 6144, 1024, bm, "inproj_gates")
    dtf = _mm_small(xn, xsn, w_t, layer, bm)
    return lru, z, xbc, qkv, gates, dtf


def _pad_rows(x, rows):
    nb, w = x.shape
    return jnp.zeros((nb, rows, w), x.dtype).at[:, 0].set(x).reshape(nb * rows, w)


def _tail8(buf):
    return jnp.pad(buf, ((0, 0), (5, 0), (0, 0)))


def kernel(x_prompt, x_sample, cache_k, cache_v, cache_logf, state_lru_h, state_lru_conv, state_ssm, state_ssm_conv, page_table, norm1_g, w_in, b_gate, lru_conv_w, lru_conv_b, lru_wa, lru_ba, lru_wx, lru_bx, lru_lambda, ssm_conv_w, ssm_conv_b, ssm_dt_bias, ssm_a_log, ssm_d, ssm_norm_g, att_q_norm_g, att_k_norm_g, att_f_bias, w_branch, w_out, norm2_g, w_up, w_down):
    a = dict(b_gate=b_gate, lru_conv_w=lru_conv_w, lru_conv_b=lru_conv_b, lru_wa=lru_wa, lru_ba=lru_ba,
             lru_wx=lru_wx, lru_bx=lru_bx, lru_lambda=lru_lambda, ssm_conv_w=ssm_conv_w,
             ssm_conv_b=ssm_conv_b, ssm_dt_bias=ssm_dt_bias, ssm_a_log=ssm_a_log, ssm_d=ssm_d,
             ssm_norm_g=ssm_norm_g, att_q_norm_g=att_q_norm_g, att_k_norm_g=att_k_norm_g,
             att_f_bias=att_f_bias, norm2_g=norm2_g,
             w_branch_bf=w_branch.astype(BF16), w_out_bf=w_out.astype(BF16),
             w_up_bf=w_up.astype(BF16), w_down_bf=w_down.astype(BF16))
    nbp, seq, _ = x_prompt.shape
    nbs = x_sample.shape[0]
    mp = nbp * seq
    npg = seq // PAGE
    n_pool = cache_k.shape[1]
    consts = _constants()

    w_t = jnp.swapaxes(w_in, 1, 2)
    cache_kt = jnp.transpose(cache_k, (0, 1, 3, 4, 2)).reshape(DEPTH, n_pool, D_ATT, PAGE)
    cache_vt = jnp.transpose(cache_v, (0, 1, 3, 4, 2)).reshape(DEPTH, n_pool, D_ATT, PAGE)
    cache_lft = jnp.transpose(cache_logf, (0, 1, 3, 2))

    xp = x_prompt.reshape(mp, D_MODEL)
    xs = x_sample.reshape(nbs, D_MODEL)
    xpn = _rmsnorm(xp, norm1_g[0][None], 512)
    xsn = _rmsnorm(xs, norm1_g[0][None], nbs)

    zeros_h = jnp.zeros((nbp, 1, D_LRU), F32)
    zeros_lt = jnp.zeros((nbp, 8, D_LRU), F32)
    zeros_st = jnp.zeros((nbp, D_SSM, N_SSM), F32)
    zeros_xt = jnp.zeros((nbp, 8, D_XBC), F32)

    outs = [[] for _ in range(14)]
    kv_prev = None
    for l in range(DEPTH):
        p = _layer_params(l, a, consts)
        g_next = norm1_g[(l + 1) % DEPTH][None]

        ((lru, lru_s), (z, z_s), (xbc, xbc_s), (qkv, qkv_s), (gates, gates_s), (dtf, dtf_s)) = _in_proj(
            xpn, xsn, w_t, l, min(1024, mp))

        qs, ks, lfs = _prep_s(qkv_s, dtf_s, p)
        vs = qkv_s[:, 2 * D_ATT:]
        bcast = lambda v: jnp.broadcast_to(v[:, :, None], (nbs, v.shape[1], PAGE))
        qcol = bcast(qs)
        y_lru, h_last = _lru_branch(lru, p, zeros_h, zeros_lt, nbp, seq, 512, 511)
        y_ssm, st_last, s_mat, lfg = _ssd_branch(xbc, z, dtf, p, zeros_st, zeros_xt, nbp, seq, PAGE,
                                                 decode=(page_table, qcol, cache_kt, cache_lft, l))
        pmat, pnew = _dec_softmax(s_mat, lfg, qcol, bcast(ks), bcast(lfs[:, :H_ATT]), p)
        q_aug, kta, vta, kt_all, vt_all, lft_all = _prep(qkv, dtf, p, nbp, seq, l, kv_prev)
        kv_prev = (kt_all, vt_all, lft_all)
        y_att, o_s = _attention(q_aug, kta, vta, nbp, seq,
                                decode=(page_table, pmat, pnew, bcast(vs), cache_vt, l))
        y_att_s = o_s[:, :, 0].astype(BF16)
        outs[6].append(h_last.reshape(nbp, D_LRU))
        outs[8].append(lru.reshape(nbp, seq, 2 * D_LRU)[:, seq - 3:, :D_LRU])
        outs[10].append(st_last.reshape(nbp, H_SSM, P_SSM, N_SSM))
        outs[12].append(xbc.reshape(nbp, seq, D_XBC)[:, seq - 3:, :])

        y_lru_s, h_s = _lru_branch(_pad_rows(lru_s, 8), p, state_lru_h[l][:, None, :],
                                   _tail8(state_lru_conv[l]), nbs, 8, 8, 0)
        y_ssm_s, st_s = _ssd_branch(_pad_rows(xbc_s, PAGE), _pad_rows(z_s, PAGE), _pad_rows(dtf_s, PAGE), p,
                                    state_ssm[l].reshape(nbs, D_SSM, N_SSM), _tail8(state_ssm_conv[l]),
                                    nbs, PAGE, 1)

        x1, x1n, x1s, x1sn = _merge((y_lru, y_ssm, y_att, gates, xp),
                                    (y_lru_s[::8], y_ssm_s[::PAGE], y_att_s, gates_s, xs), p, l, 256)
        xp, xpn, xs, xsn = _mlp((x1n, x1), (x1sn, x1s), p, l, g_next, min(1024, mp))
        outs[3].append(ks.reshape(nbs, 1, H_ATT, HD))
        outs[4].append(vs.reshape(nbs, 1, H_ATT, HD))
        outs[5].append(lfs[:, :H_ATT].reshape(nbs, 1, H_ATT))
        outs[7].append(h_s.reshape(nbs, D_LRU))
        outs[9].append(jnp.concatenate([state_lru_conv[l][:, 1:], lru_s[:, None, :D_LRU]], axis=1))
        outs[11].append(st_s.reshape(nbs, H_SSM, P_SSM, N_SSM))
        outs[13].append(jnp.concatenate([state_ssm_conv[l][:, 1:], xbc_s[:, None, :]], axis=1))

    st = [jnp.stack(o) if o else None for o in outs]
    kt_all, vt_all, lft_all = kv_prev
    new_k_p = jnp.transpose(kt_all.reshape(DEPTH, nbp, npg, H_ATT, HD, PAGE), (0, 1, 2, 5, 3, 4))
    new_v_p = jnp.transpose(vt_all.reshape(DEPTH, nbp, npg, H_ATT, HD, PAGE), (0, 1, 2, 5, 3, 4))
    new_lf_p = jnp.transpose(lft_all, (0, 1, 2, 4, 3))
    return (xp.reshape(nbp, seq, D_MODEL), xs.reshape(nbs, 1, D_MODEL),
            new_k_p, new_v_p, new_lf_p, st[3], st[4], st[5],
            st[6], st[7], st[8], st[9], st[10], st[11], st[12], st[13])
```

```python
import functools

import jax
import jax.numpy as jnp
from jax import lax
from jax.experimental import pallas as pl
from jax.experimental.pallas import tpu as pltpu

F32 = jnp.float32
BF16 = jnp.bfloat16

D_MODEL = 2048
DEPTH = 4
PAGE = 128
CONV_W = 4
D_LRU = 1024
LRU_BLOCKS = 16
LRU_BW = 64
LRU_C = 8.0
H_SSM = 16
P_SSM = 64
D_SSM = 1024
G_SSM = 2
N_SSM = 128
D_XBC = D_SSM + 2 * G_SSM * N_SSM
H_ATT = 16
HD = 64
D_ATT = 1024
N_BRANCH = 3
D_FF = 4 * D_MODEL
EPS = 1e-6
NEG = -1e30

OFF_LRU = 0
OFF_Z = 2048
OFF_XBC = 3072
OFF_DT = 4608
OFF_QKV = 4624
OFF_F = 7696
OFF_GATES = 7712

VMEM_LIMIT = 56 * 1024 * 1024


def _cparams(sem):
    return pltpu.CompilerParams(dimension_semantics=sem, vmem_limit_bytes=VMEM_LIMIT)


def _dot(a, b):
    return jnp.dot(a, b, preferred_element_type=F32)


def _dot_nt(a, b):
    return lax.dot_general(a, b, (((1,), (1,)), ((), ())), preferred_element_type=F32)


def _split3(x):
    hi = x.astype(BF16)
    r = x - hi.astype(F32)
    mid = r.astype(BF16)
    lo = (r - mid.astype(F32)).astype(BF16)
    return hi, mid, lo


def _dot3_right(x, m01):
    hi, mid, lo = _split3(x)
    return _dot(hi, m01) + _dot(mid, m01) + _dot(lo, m01)


def _dot3_left(m01, x):
    hi, mid, lo = _split3(x)
    return _dot(m01, hi) + _dot(m01, mid) + _dot(m01, lo)


def _sigmoid(x):
    return 0.5 * jnp.tanh(0.5 * x) + 0.5


def _softplus(x):
    return jnp.maximum(x, 0.0) + jnp.log1p(jnp.exp(-jnp.abs(x)))


def _rms(x, g):
    ms = jnp.mean(x * x, axis=-1, keepdims=True)
    return x * lax.rsqrt(ms + EPS) * g


def _rms_kernel(x_ref, g_ref, o_ref):
    o_ref[...] = _rms(x_ref[...], g_ref[...]).astype(o_ref.dtype)


def _rmsnorm(x, g, bm):
    m, d = x.shape
    return pl.pallas_call(
        _rms_kernel,
        out_shape=jax.ShapeDtypeStruct((m, d), BF16),
        grid=(m // bm,),
        in_specs=[pl.BlockSpec((bm, d), lambda i: (i, 0)),
                  pl.BlockSpec((1, d), lambda i: (0, 0))],
        out_specs=pl.BlockSpec((bm, d), lambda i: (i, 0)),
        compiler_params=_cparams(("arbitrary",)),
        name="rmsnorm",
    )(x, g)


def _mm_nt_kernel(x_ref, xs_ref, w_ref, o_ref, os_ref, wb_ref):
    @pl.when(pl.program_id(1) == 0)
    def _():
        wb_ref[...] = w_ref[...].astype(BF16)
        os_ref[...] = _dot_nt(xs_ref[...], wb_ref[...])

    o_ref[...] = _dot_nt(x_ref[...], wb_ref[...])


def _mm_nt(x, xs, w_t, layer, off, n, bn, bm, name):
    m, k = x.shape
    ms = xs.shape[0]
    return pl.pallas_call(
        _mm_nt_kernel,
        out_shape=(jax.ShapeDtypeStruct((m, n), F32), jax.ShapeDtypeStruct((ms, n), F32)),
        grid=(n // bn, m // bm),
        in_specs=[pl.BlockSpec((bm, k), lambda j, i: (i, 0)),
                  pl.BlockSpec((ms, k), lambda j, i: (0, 0)),
                  pl.BlockSpec((None, pl.Element(bn), pl.Element(k)),
                               lambda j, i: (layer, pl.multiple_of(off + j * bn, 8), 0))],
        out_specs=(pl.BlockSpec((bm, bn), lambda j, i: (i, j)),
                   pl.BlockSpec((ms, bn), lambda j, i: (0, j))),
        scratch_shapes=[pltpu.VMEM((bn, k), BF16)],
        compiler_params=_cparams(("arbitrary", "arbitrary")),
        name=name,
    )(x, xs, w_t)


def _mm_small_kernel(x_ref, xs_ref, wa_ref, wb_ref, o_ref, os_ref):
    wa = wa_ref[...].astype(BF16)
    wb = wb_ref[...].astype(BF16)
    x = x_ref[...]
    o_ref[:, 0:128] = _dot_nt(x, wa)
    o_ref[:, 128:256] = _dot_nt(x, wb)

    @pl.when(pl.program_id(0) == 0)
    def _():
        xs = xs_ref[...]
        os_ref[:, 0:128] = _dot_nt(xs, wa)
        os_ref[:, 128:256] = _dot_nt(xs, wb)


def _mm_small(x, xs, w_t, layer, bm):
    m, k = x.shape
    ms = xs.shape[0]
    return pl.pallas_call(
        _mm_small_kernel,
        out_shape=(jax.ShapeDtypeStruct((m, 256), F32), jax.ShapeDtypeStruct((ms, 256), F32)),
        grid=(m // bm,),
        in_specs=[pl.BlockSpec((bm, k), lambda i: (i, 0)),
                  pl.BlockSpec((ms, k), lambda i: (0, 0)),
                  pl.BlockSpec((None, pl.Element(128), pl.Element(k)), lambda i: (layer, OFF_DT, 0)),
                  pl.BlockSpec((None, pl.Element(128), pl.Element(k)), lambda i: (layer, OFF_F, 0))],
        out_specs=(pl.BlockSpec((bm, 256), lambda i: (i, 0)),
                   pl.BlockSpec((ms, 256), lambda i: (0, 0))),
        compiler_params=_cparams(("arbitrary",)),
        name="inproj_small",
    )(x, xs, w_t, w_t)


def _causal_conv(ext_ref, x, cw, cb, rows):
    ext_ref[8:8 + rows, :] = x
    out = cb + ext_ref[5:5 + rows, :] * cw[0:1]
    out = out + ext_ref[6:6 + rows, :] * cw[1:2]
    out = out + ext_ref[7:7 + rows, :] * cw[2:3]
    out = out + x * cw[3:4]
    ext_ref[0:8, :] = ext_ref[rows:rows + 8, :]
    return out


def _lru_kernel(u_ref, g_ref, cw_ref, cb_ref, wa_ref, wx_ref, ba_ref, bx_ref, lam_ref,
                h0_ref, tail0_ref, y_ref, hl_ref, ext_ref, hc_ref,
                *, tt, last_row):
    t = pl.program_id(1)
    nt = pl.num_programs(1)

    @pl.when(t == 0)
    def _():
        ext_ref[0:8, :] = tail0_ref[...]
        hc_ref[...] = h0_ref[...]

    uc = _causal_conv(ext_ref, u_ref[...], cw_ref[...], cb_ref[...], tt)
    ucb = uc.astype(BF16)
    ra = jnp.concatenate([_dot(ucb[:, 256 * c:256 * (c + 1)], wa_ref[c]) for c in range(4)], axis=1)
    rx = jnp.concatenate([_dot(ucb[:, 256 * c:256 * (c + 1)], wx_ref[c]) for c in range(4)], axis=1)
    r = _sigmoid(ra + ba_ref[...])
    gi = _sigmoid(rx + bx_ref[...])
    log_a = (-LRU_C * _softplus(-lam_ref[...])) * r
    a = jnp.exp(log_a)
    th = jnp.tanh(log_a)
    one_m_a2 = -2.0 * th / (1.0 - th)
    root = jnp.where(one_m_a2 > 0.0, one_m_a2 * lax.rsqrt(one_m_a2), 0.0)
    bvec = root * (gi * uc)

    row8 = lax.broadcasted_iota(jnp.int32, (tt, D_LRU), 0) & 7
    for k in (1, 2, 4):
        inside = row8 >= k
        a_prev = jnp.where(inside, pltpu.roll(a, k, 0), 1.0)
        b_prev = jnp.where(inside, pltpu.roll(bvec, k, 0), 0.0)
        bvec = bvec + a * b_prev
        a = a * a_prev
    h_in = hc_ref[...]
    groups = []
    for r in range(tt // 8):
        hg = bvec[8 * r:8 * (r + 1)] + a[8 * r:8 * (r + 1)] * h_in
        groups.append(hg)
        h_in = hg[7:8]
    hc_ref[...] = h_in
    h = jnp.concatenate(groups, axis=0)
    g = g_ref[...]
    gelu = 0.5 * g * (1.0 + jnp.tanh(0.7978845608028654 * (g + 0.044715 * (g * g * g))))
    y_ref[...] = (h * gelu).astype(y_ref.dtype)

    @pl.when(t == nt - 1)
    def _():
        hl_ref[...] = h[last_row:last_row + 1]


def _lru_branch(lru, p, h0, tail0, nb, seq, tt, last_row):
    nt = seq // tt
    kern = functools.partial(_lru_kernel, tt=tt, last_row=last_row)
    vec = lambda: pl.BlockSpec((1, D_LRU), lambda b, t: (0, 0))
    return pl.pallas_call(
        kern,
        out_shape=(jax.ShapeDtypeStruct((nb * seq, D_LRU), BF16),
                   jax.ShapeDtypeStruct((nb, 1, D_LRU), F32)),
        grid=(nb, nt),
        in_specs=[pl.BlockSpec((tt, D_LRU), lambda b, t: (b * nt + t, 0)),
                  pl.BlockSpec((tt, D_LRU), lambda b, t: (b * nt + t, 1)),
                  pl.BlockSpec((CONV_W, D_LRU), lambda b, t: (0, 0)),
                  vec(),
                  pl.BlockSpec((4, 256, 256), lambda b, t: (0, 0, 0)),
                  pl.BlockSpec((4, 256, 256), lambda b, t: (0, 0, 0)),
                  vec(), vec(), vec(),
                  pl.BlockSpec((None, 1, D_LRU), lambda b, t: (b, 0, 0)),
                  pl.BlockSpec((None, 8, D_LRU), lambda b, t: (b, 0, 0))],
        out_specs=(pl.BlockSpec((tt, D_LRU), lambda b, t: (b * nt + t, 0)),
                   pl.BlockSpec((None, 1, D_LRU), lambda b, t: (b, 0, 0))),
        scratch_shapes=[pltpu.VMEM((tt + 8, D_LRU), F32),
                        pltpu.VMEM((1, D_LRU), F32)],
        compiler_params=_cparams(("arbitrary", "arbitrary")),
        name="rglru",
    )(lru, lru, p["lru_conv_w"], p["lru_conv_b"], p["wa_bd"], p["wx_bd"],
      p["lru_ba"], p["lru_bx"], p["lru_lambda"], h0, tail0)


def _ssd_kernel(*refs, n_valid, ride):
    if ride:
        refs = refs[1:]
        _decode_scores_pages(refs[13], refs[14:14 + ride], refs[14 + ride:14 + 2 * ride],
                             refs[16 + 2 * ride], refs[17 + 2 * ride])
        refs = refs[:13] + refs[14 + 2 * ride:16 + 2 * ride] + refs[18 + 2 * ride:]
    (xbc_ref, z_ref, dtf_ref, cw_ref, cb_ref, dtb_ref, alog_ref, dx_ref, ng_ref,
     tri_ref, e_ref, st0_ref, tail0_ref, y_ref, stf_ref, ext_ref, st_ref) = refs
    c = pl.program_id(1)
    nc = pl.num_programs(1)
    q = PAGE

    @pl.when(c == 0)
    def _():
        ext_ref[0:8, :] = tail0_ref[...]
        st_ref[...] = st0_ref[...].T

    conv = _causal_conv(ext_ref, xbc_ref[...], cw_ref[...], cb_ref[...], q)
    xc = conv * _sigmoid(conv)
    xs = xc[:, 0:D_SSM]
    bm = xc[:, D_SSM:D_SSM + 256]
    cm = xc[:, D_SSM + 256:D_SSM + 512]

    row = lax.broadcasted_iota(jnp.int32, (q, q), 0)
    col = lax.broadcasted_iota(jnp.int32, (q, q), 1)
    dt = _softplus(dtf_ref[:, 0:128] + dtb_ref[...])
    if n_valid < q:
        dt = jnp.where(row < n_valid, dt, 0.0)
    a_neg = -jnp.exp(alog_ref[...])
    dta = dt * a_neg
    cs = _dot3_left(tri_ref[...], dta)
    cs_t = cs.T
    dt_t = dt.T
    cs_last = cs[q - 1:q, :]
    e01 = e_ref[...]
    exp_cs_x = _dot3_right(jnp.exp(cs), e01)
    ws_x = _dot3_right(jnp.exp(cs_last - cs) * dt, e01)
    dec_x = _dot3_right(jnp.broadcast_to(jnp.exp(cs_last), (8, 128)), e01)[0:1]
    dx = dx_ref[...]
    lane_lo = col < 64
    tril = col <= row

    ys = []
    for g in range(G_SSM):
        bg = bm[:, 128 * g:128 * (g + 1)]
        cgb = cm[:, 128 * g:128 * (g + 1)].astype(BF16)
        gl = slice(512 * g, 512 * (g + 1))
        cb_mat = _dot_nt(cgb, bg.astype(BF16))
        st_g = st_ref[:, gl]
        yoff = _dot(cgb, st_g.astype(BF16)) * exp_cs_x[:, gl]
        xw = (xs[:, gl] * ws_x[:, gl]).astype(BF16)
        st_ref[:, gl] = st_g * dec_x[:, gl] + _dot(bg.T.astype(BF16), xw)
        for jj in range(4):
            ws = []
            for a in range(2):
                h = 8 * g + 2 * jj + a
                seg = cs[:, h:h + 1] - cs_t[h:h + 1, :]
                lm = jnp.exp(jnp.where(tril, seg, NEG))
                ws.append((cb_mat * lm * dt_t[h:h + 1, :]).astype(BF16))
            wcat = jnp.concatenate(ws, axis=1)
            ll = slice(512 * g + 128 * jj, 512 * g + 128 * (jj + 1))
            x2 = xs[:, ll]
            xst = jnp.concatenate([jnp.where(lane_lo, x2, 0.0), jnp.where(lane_lo, 0.0, x2)],
                                  axis=0).astype(BF16)
            ys.append(_dot(wcat, xst) + yoff[:, 128 * jj:128 * (jj + 1)] + dx[:, ll] * x2)
    y = jnp.concatenate(ys, axis=1)
    z = z_ref[...]
    y_ref[...] = _rms(y * (z * _sigmoid(z)), ng_ref[...]).astype(y_ref.dtype)

    @pl.when(c == nc - 1)
    def _():
        stf_ref[...] = st_ref[...].T


def _rider_pages(page_table, n_steps):
    nbs, npg = page_table.shape
    steps_per_row = n_steps // nbs
    assert steps_per_row * nbs == n_steps and npg % steps_per_row == 0
    return steps_per_row, npg // steps_per_row


def _ssd_branch(xbc, z, dtf, p, st0, tail0, nb, seq, n_valid, decode=None):
    nc = seq // PAGE
    const = lambda shape: pl.BlockSpec(shape, lambda b, c, *_: (0,) * len(shape))
    in_specs = [pl.BlockSpec((PAGE, D_XBC), lambda b, c, *_: (b * nc + c, 0)),
                pl.BlockSpec((PAGE, D_SSM), lambda b, c, *_: (b * nc + c, 0)),
                pl.BlockSpec((PAGE, 256), lambda b, c, *_: (b * nc + c, 0)),
                const((CONV_W, D_XBC)), const((1, D_XBC)),
                const((1, 128)), const((1, 128)), const((1, D_SSM)), const((1, D_SSM)),
                const((128, 128)), const((128, D_SSM)),
                pl.BlockSpec((None, D_SSM, N_SSM), lambda b, c, *_: (b, 0, 0)),
                pl.BlockSpec((None, 8, D_XBC), lambda b, c, *_: (b, 0, 0))]
    out_specs = [pl.BlockSpec((PAGE, D_SSM), lambda b, c, *_: (b * nc + c, 0)),
                 pl.BlockSpec((None, D_SSM, N_SSM), lambda b, c, *_: (b, 0, 0))]
    out_shape = [jax.ShapeDtypeStruct((nb * seq, D_SSM), BF16),
                 jax.ShapeDtypeStruct((nb, D_SSM, N_SSM), F32)]
    args = [xbc, z, dtf, p["ssm_conv_w"], p["ssm_conv_b"], p["dt_bias128"], p["a_log128"],
            p["d_x"], p["ssm_norm_g"], p["tri_incl"], p["expand"], st0, tail0]
    scratch = [pltpu.VMEM((PAGE + 8, D_XBC), F32), pltpu.VMEM((N_SSM, D_SSM), F32)]
    ride = 0
    if decode is not None:
        page_table, qcol, cache_kt, cache_lft, layer = decode
        nbs, npg = page_table.shape
        spr, ride = _rider_pages(page_table, nb * nc)
        row = lambda b, c: (b * nc + c) // spr
        grp = lambda b, c: (b * nc + c) % spr
        pg = lambda g: (lambda b, c, pt: (layer, pt[row(b, c), grp(b, c) * ride + g], 0, 0))
        in_specs += [pl.BlockSpec((None, D_ATT, PAGE), lambda b, c, pt: (row(b, c), 0, 0))]
        in_specs += [pl.BlockSpec((None, None, D_ATT, PAGE), pg(g)) for g in range(ride)]
        in_specs += [pl.BlockSpec((None, None, H_ATT, PAGE), pg(g)) for g in range(ride)]
        out_specs += [pl.BlockSpec((None, ride, H_ATT, PAGE), lambda b, c, pt: (row(b, c), grp(b, c), 0, 0))] * 2
        out_shape += [jax.ShapeDtypeStruct((nbs, npg, H_ATT, PAGE), F32)] * 2
        args = [page_table] + args + [qcol] + [cache_kt] * ride + [cache_lft] * ride
    grid_spec = pltpu.PrefetchScalarGridSpec(
        num_scalar_prefetch=1 if decode is not None else 0,
        grid=(nb, nc), in_specs=in_specs, out_specs=tuple(out_specs), scratch_shapes=scratch)
    return pl.pallas_call(
        functools.partial(_ssd_kernel, n_valid=n_valid, ride=ride),
        out_shape=tuple(out_shape),
        grid_spec=grid_spec,
        compiler_params=_cparams(("arbitrary", "arbitrary")),
        name="ssd",
    )(*args)


def _head_rms(x, bd, g):
    x2 = x * x
    hi = x2.astype(BF16)
    lo = (x2 - hi.astype(F32)).astype(BF16)
    ss = jnp.concatenate(
        [_dot(hi[:, 256 * c:256 * (c + 1)], bd) + _dot(lo[:, 256 * c:256 * (c + 1)], bd)
         for c in range(4)], axis=1)
    return x * lax.rsqrt(ss * (1.0 / HD) + EPS) * g


def _log_sigmoid(x):
    return -_softplus(-x)


AUG = 16
LOG2E = 1.4426950408889634


def _prep_kernel(*refs, aliased):
    (qkv_ref, dtf_ref, gq_ref, gk_ref, fb_ref, bd_ref, triu_ref, ones_ref, sel_ref, augc_ref) = refs[:10]
    refs = refs[10 + (3 if aliased else 0):]
    q_ref, kta_ref, vta_ref, kt_ref, vt_ref, lft_ref, car_ref = refs

    @pl.when(pl.program_id(1) == 0)
    def _():
        car_ref[...] = jnp.zeros_like(car_ref)

    bd = bd_ref[...]
    q = qkv_ref[:, 0:D_ATT]
    k = qkv_ref[:, D_ATT:2 * D_ATT]
    v = qkv_ref[:, 2 * D_ATT:3 * D_ATT]
    kt = _head_rms(k, bd, gk_ref[...]).T
    kt_ref[...] = kt
    vt = v.T
    vt_ref[...] = vt

    lf_t = _log_sigmoid(dtf_ref[:, 128:256] + fb_ref[...]).T
    lft_ref[...] = lf_t[0:H_ATT]
    f_t = _dot3_right(lf_t, triu_ref[...]) + car_ref[...]
    car_ref[...] = car_ref[...] + _dot3_right(lf_t, ones_ref[...])
    fs_t = f_t * LOG2E
    hi_t, mid_t, lo_t = _split3(fs_t)

    qn = _head_rms(q, bd, gq_ref[...]) * (HD ** -0.5 * LOG2E)
    lane_lo = lax.broadcasted_iota(jnp.int32, (PAGE, 128), 1) < 64
    blocks = []
    for m in range(H_ATT // 2):
        x2 = qn[:, 128 * m:128 * (m + 1)]
        blocks.append(jnp.where(lane_lo, x2, 0.0))
        blocks.append(jnp.where(lane_lo, pltpu.roll(x2, 64, 1), 0.0))
    parts = jnp.concatenate(_split3(fs_t.T), axis=1)
    q_ref[...] = (jnp.concatenate(blocks, axis=1) + _dot(parts, sel_ref[...]) + augc_ref[...]).astype(BF16)

    row = lax.broadcasted_iota(jnp.int32, (AUG, PAGE), 0)
    zpad = jnp.zeros((128 - HD - AUG, PAGE), BF16)
    v_aug = jnp.where(row == 0, 1.0, 0.0).astype(BF16)
    hi_f, mid_f, lo_f = hi_t.astype(F32), mid_t.astype(F32), lo_t.astype(F32)
    for h in range(H_ATT):
        k_aug = jnp.where(row < 3, 1.0,
                          jnp.where(row == 3, -hi_f[h:h + 1],
                                    jnp.where(row == 4, -mid_f[h:h + 1],
                                              jnp.where(row == 5, -lo_f[h:h + 1], 0.0)))).astype(BF16)
        kta_ref[h] = jnp.concatenate([kt[HD * h:HD * (h + 1)].astype(BF16), k_aug, zpad], axis=0)
        vta_ref[h] = jnp.concatenate([vt[HD * h:HD * (h + 1)].astype(BF16), v_aug, zpad], axis=0)


def _prep(qkv, dtf, p, nb, seq, layer, prev):
    npg = seq // PAGE
    m = nb * seq
    const = lambda shape: pl.BlockSpec(shape, lambda b, c: (0,) * len(shape))
    headpage = pl.BlockSpec((None, None, H_ATT, 128, PAGE), lambda b, c: (b, c, 0, 0, 0))
    stacked = lambda rows: pl.BlockSpec((None, None, None, rows, PAGE), lambda b, c: (layer, b, c, 0, 0))
    args = [qkv, dtf, p["gq_x"], p["gk_x"], p["f_bias128"], p["bd_ones"], p["tri_incl_t"], p["ones128"],
            p["sel_big"], p["aug_const"]]
    in_specs = [pl.BlockSpec((PAGE, 3 * D_ATT), lambda b, c: (b * npg + c, 0)),
                pl.BlockSpec((PAGE, 256), lambda b, c: (b * npg + c, 0)),
                const((1, D_ATT)), const((1, D_ATT)), const((1, 128)),
                const((256, 256)), const((128, 128)), const((128, 128)),
                const((3 * 128, H_ATT * 128)), const((1, H_ATT * 128))]
    aliases = {}
    if prev is not None:
        args += list(prev)
        in_specs += [pl.BlockSpec(memory_space=pl.ANY)] * 3
        aliases = {10: 3, 11: 4, 12: 5}
    return pl.pallas_call(
        functools.partial(_prep_kernel, aliased=prev is not None),
        out_shape=(jax.ShapeDtypeStruct((m, H_ATT * 128), BF16),
                   jax.ShapeDtypeStruct((nb, npg, H_ATT, 128, PAGE), BF16),
                   jax.ShapeDtypeStruct((nb, npg, H_ATT, 128, PAGE), BF16),
                   jax.ShapeDtypeStruct((DEPTH, nb, npg, D_ATT, PAGE), F32),
                   jax.ShapeDtypeStruct((DEPTH, nb, npg, D_ATT, PAGE), F32),
                   jax.ShapeDtypeStruct((DEPTH, nb, npg, H_ATT, PAGE), F32)),
        grid=(nb, npg),
        in_specs=in_specs,
        out_specs=(pl.BlockSpec((PAGE, H_ATT * 128), lambda b, c: (b * npg + c, 0)),
                   headpage, headpage, stacked(D_ATT), stacked(D_ATT), stacked(H_ATT)),
        scratch_shapes=[pltpu.VMEM((128, 128), F32)],
        input_output_aliases=aliases,
        compiler_params=_cparams(("arbitrary", "arbitrary")),
        name="attn_prep",
    )(*args)


def _attn_kernel(*refs, tq, pages, ride, steps_per_row):
    if ride:
        q_ref, kt_ref, vt_ref, p_ref, pn_ref, vn_ref = refs[1:7]
        v_refs = refs[7:7 + ride]
        o_ref, od_ref, acc_ref = refs[7 + ride:]
        step = ((pl.program_id(0) * pl.num_programs(1) + pl.program_id(1)) * pl.num_programs(2)
                + pl.program_id(2))
        grp = step % steps_per_row
        _decode_values_pages(p_ref, pn_ref, vn_ref, v_refs, od_ref, acc_ref,
                             grp == 0, grp == steps_per_row - 1)
    else:
        q_ref, kt_ref, vt_ref, o_ref = refs
    i = pl.program_id(2)
    tk = pages * PAGE
    assert tq == tk
    qs = [q_ref[:, 0:128], q_ref[:, 128:256]]

    def causal(n_pages):
        nk = n_pages * PAGE
        col = lax.broadcasted_iota(jnp.int32, (tq, nk), 1)
        row = lax.broadcasted_iota(jnp.int32, (tq, nk), 0)
        return col <= row + (nk - tq)

    def tile(page0, n_pages, carry, masked):
        out = []
        for a in range(2):
            m_old, acc = carry[a]
            kt = jnp.concatenate([kt_ref[page0 + pg, a] for pg in range(n_pages)], axis=1)
            vt = jnp.concatenate([vt_ref[page0 + pg, a] for pg in range(n_pages)], axis=1)
            s = _dot(qs[a], kt)
            if masked:
                s = jnp.where(causal(n_pages), s, NEG)
            m_new = jnp.maximum(m_old, jnp.max(s, axis=-1, keepdims=True))
            pr = jnp.exp2(s - m_new).astype(BF16)
            acc = acc * jnp.exp2(m_old - m_new) + _dot_nt(pr, vt)
            out.append((m_new, acc))
        return tuple(out)

    init = tuple((jnp.full((tq, 1), NEG, F32), jnp.zeros((tq, 128), F32)) for _ in range(2))
    carry = lax.fori_loop(0, i // 2, lambda kk, c: tile(kk * 2 * pages, 2 * pages, c, False), init)
    (_, acc0), (_, acc1) = lax.cond(i % 2 == 1,
                                    lambda c: tile((i - 1) * pages, 2 * pages, c, True),
                                    lambda c: tile(i * pages, pages, c, True), carry)
    o0 = acc0 * (1.0 / acc0[:, HD:HD + 1])
    o1 = acc1 * (1.0 / acc1[:, HD:HD + 1])
    lane_lo = lax.broadcasted_iota(jnp.int32, (tq, 128), 1) < HD
    o_ref[...] = jnp.where(lane_lo, o0, pltpu.roll(o1, HD, 1)).astype(o_ref.dtype)


def _attention(q_aug, kta, vta, nb, seq, tq=512, decode=None):
    nq = seq // tq
    npg = seq // PAGE
    nj = H_ATT // 2
    in_specs = [pl.BlockSpec((tq, 256), lambda b, j, i, *_: (b * nq + i, j)),
                pl.BlockSpec((None, npg, 2, 128, PAGE), lambda b, j, i, *_: (b, 0, j, 0, 0)),
                pl.BlockSpec((None, npg, 2, 128, PAGE), lambda b, j, i, *_: (b, 0, j, 0, 0))]
    out_specs = [pl.BlockSpec((tq, 128), lambda b, j, i, *_: (b * nq + i, j))]
    out_shape = [jax.ShapeDtypeStruct((nb * seq, D_ATT), BF16)]
    args = [q_aug, kta, vta]
    scratch = []
    ride, spr = 0, 1
    if decode is not None:
        page_table, pmat, pnew, vncol, cache_vt, layer = decode
        nbs = page_table.shape[0]
        spr, ride = _rider_pages(page_table, nb * nj * nq)
        step = lambda b, j, i: (b * nj + j) * nq + i
        row = lambda b, j, i: step(b, j, i) // spr
        grp = lambda b, j, i: step(b, j, i) % spr
        pg = lambda g: (lambda b, j, i, pt: (layer, pt[row(b, j, i), grp(b, j, i) * ride + g], 0, 0))
        in_specs += [pl.BlockSpec((None, ride, H_ATT, PAGE), lambda b, j, i, pt: (row(b, j, i), grp(b, j, i), 0, 0)),
                     pl.BlockSpec((None, H_ATT, PAGE), lambda b, j, i, pt: (row(b, j, i), 0, 0)),
                     pl.BlockSpec((None, D_ATT, PAGE), lambda b, j, i, pt: (row(b, j, i), 0, 0))]
        in_specs += [pl.BlockSpec((None, None, D_ATT, PAGE), pg(g)) for g in range(ride)]
        out_specs += [pl.BlockSpec((None, D_ATT, PAGE), lambda b, j, i, pt: (row(b, j, i), 0, 0))]
        out_shape += [jax.ShapeDtypeStruct((nbs, D_ATT, PAGE), F32)]
        args = [page_table] + args + [pmat, pnew, vncol] + [cache_vt] * ride
        scratch = [pltpu.VMEM((D_ATT, PAGE), F32)]
    grid_spec = pltpu.PrefetchScalarGridSpec(
        num_scalar_prefetch=1 if decode is not None else 0,
        grid=(nb, nj, nq), in_specs=in_specs, out_specs=tuple(out_specs), scratch_shapes=scratch)
    out = pl.pallas_call(
        functools.partial(_attn_kernel, tq=tq, pages=tq // PAGE, ride=ride, steps_per_row=spr),
        out_shape=tuple(out_shape),
        grid_spec=grid_spec,
        compiler_params=_cparams(("arbitrary", "arbitrary", "arbitrary")),
        name="fox_attention",
    )(*args)
    return out if decode is not None else out[0]


def _merge_kernel(bg_ref, wb_ref, wo_ref, g2_ref,
                  y0_ref, y1_ref, y2_ref, gates_ref, x_ref,
                  sy0_ref, sy1_ref, sy2_ref, sgates_ref, sx_ref,
                  x1_ref, xn_ref, sx1_ref, sxn_ref):
    def rows(ys, gates, x, x1_out, xn_out):
        merged = None
        for n in range(N_BRANCH):
            pb = _dot(ys[n][...], wb_ref[n])
            gate = _sigmoid(gates[:, D_MODEL * n:D_MODEL * (n + 1)] + bg_ref[n:n + 1, :])
            merged = gate * pb if merged is None else merged + gate * pb
        x1 = x[...] + _dot(merged.astype(BF16), wo_ref[...])
        x1_out[...] = x1
        xn_out[...] = _rms(x1, g2_ref[...]).astype(xn_out.dtype)

    rows((y0_ref, y1_ref, y2_ref), gates_ref, x_ref, x1_ref, xn_ref)

    @pl.when(pl.program_id(0) == 0)
    def _():
        rows((sy0_ref, sy1_ref, sy2_ref), sgates_ref, sx_ref, sx1_ref, sxn_ref)


def _merge(prompt, sample, p, layer, bm):
    m = prompt[4].shape[0]
    ms = sample[4].shape[0]
    row = lambda w: pl.BlockSpec((bm, w), lambda i: (i, 0))
    srow = lambda w: pl.BlockSpec((ms, w), lambda i: (0, 0))
    widths = (D_LRU, D_SSM, D_ATT, N_BRANCH * D_MODEL, D_MODEL)
    return pl.pallas_call(
        _merge_kernel,
        out_shape=(jax.ShapeDtypeStruct((m, D_MODEL), F32),
                   jax.ShapeDtypeStruct((m, D_MODEL), BF16),
                   jax.ShapeDtypeStruct((ms, D_MODEL), F32),
                   jax.ShapeDtypeStruct((ms, D_MODEL), BF16)),
        grid=(m // bm,),
        in_specs=[pl.BlockSpec((N_BRANCH, D_MODEL), lambda i: (0, 0)),
                  pl.BlockSpec((None, N_BRANCH, D_LRU, D_MODEL), lambda i: (layer, 0, 0, 0),
                               pipeline_mode=pl.Buffered(1)),
                  pl.BlockSpec((None, D_MODEL, D_MODEL), lambda i: (layer, 0, 0),
                               pipeline_mode=pl.Buffered(1)),
                  pl.BlockSpec((1, D_MODEL), lambda i: (0, 0))]
                 + [row(w) for w in widths] + [srow(w) for w in widths],
        out_specs=(row(D_MODEL), row(D_MODEL), srow(D_MODEL), srow(D_MODEL)),
        compiler_params=_cparams(("arbitrary",)),
        name="merge_out",
    )(p["b_gate"], p["w_branch_bf"], p["w_out_bf"], p["norm2_g"], *prompt, *sample)


def _mlp_kernel(wu_ref, wd_ref, gn_ref, xn_ref, x1_ref, sxn_ref, sx1_ref, o_ref, on_ref, so_ref, son_ref):
    i = pl.program_id(0)
    f = pl.program_id(1)
    nf = pl.num_programs(1)

    def rows(xn, x1, out, out_n):
        @pl.when(f == 0)
        def _():
            out[...] = x1[...]

        h = jnp.maximum(_dot(xn[...], wu_ref[...]), 0.0)
        out[...] += _dot((h * h).astype(BF16), wd_ref[...])

        @pl.when(f == nf - 1)
        def _():
            out_n[...] = _rms(out[...], gn_ref[...]).astype(out_n.dtype)

    rows(xn_ref, x1_ref, o_ref, on_ref)

    @pl.when(i == 0)
    def _():
        rows(sxn_ref, sx1_ref, so_ref, son_ref)


def _mlp(prompt, sample, p, layer, g_next, bm, bf=1024):
    m = prompt[1].shape[0]
    ms = sample[1].shape[0]
    once = pl.Buffered(1)
    row = lambda: pl.BlockSpec((bm, D_MODEL), lambda i, f: (i, 0), pipeline_mode=once)
    srow = lambda: pl.BlockSpec((ms, D_MODEL), lambda i, f: (0, 0))
    return pl.pallas_call(
        _mlp_kernel,
        out_shape=(jax.ShapeDtypeStruct((m, D_MODEL), F32),
                   jax.ShapeDtypeStruct((m, D_MODEL), BF16),
                   jax.ShapeDtypeStruct((ms, D_MODEL), F32),
                   jax.ShapeDtypeStruct((ms, D_MODEL), BF16)),
        grid=(m // bm, D_FF // bf),
        in_specs=[pl.BlockSpec((None, D_MODEL, bf), lambda i, f: (layer, 0, f)),
                  pl.BlockSpec((None, bf, D_MODEL), lambda i, f: (layer, f, 0)),
                  pl.BlockSpec((1, D_MODEL), lambda i, f: (0, 0)),
                  row(), row(), srow(), srow()],
        out_specs=(row(), row(), srow(), srow()),
        compiler_params=_cparams(("arbitrary", "arbitrary")),
        name="mlp",
    )(p["w_up_bf"], p["w_down_bf"], g_next, *prompt, *sample)


def _prep_s_kernel(qkv_ref, dtf_ref, gq_ref, gk_ref, fb_ref, bd_ref, q_ref, k_ref, lf_ref):
    bd = bd_ref[...]
    q_ref[...] = _head_rms(qkv_ref[:, 0:D_ATT], bd, gq_ref[...]) * (HD ** -0.5)
    k_ref[...] = _head_rms(qkv_ref[:, D_ATT:2 * D_ATT], bd, gk_ref[...])
    lf_ref[...] = _log_sigmoid(dtf_ref[:, 128:256] + fb_ref[...])


def _prep_s(qkv, dtf, p):
    nb = qkv.shape[0]
    full = lambda a: pl.BlockSpec(a.shape, lambda: (0,) * a.ndim)
    args = (qkv, dtf, p["gq_x"], p["gk_x"], p["f_bias128"], p["bd_ones"])
    return pl.pallas_call(
        _prep_s_kernel,
        out_shape=(jax.ShapeDtypeStruct((nb, D_ATT), F32),
                   jax.ShapeDtypeStruct((nb, D_ATT), F32),
                   jax.ShapeDtypeStruct((nb, 128), F32)),
        in_specs=[full(a) for a in args],
        out_specs=(pl.BlockSpec((nb, D_ATT), lambda: (0, 0)),
                   pl.BlockSpec((nb, D_ATT), lambda: (0, 0)),
                   pl.BlockSpec((nb, 128), lambda: (0, 0))),
        name="attn_prep_sample",
    )(*args)


def _decode_scores_pages(qc_ref, k_refs, lf_refs, s_ref, lfo_ref):
    for h in range(H_ATT):
        rows = slice(HD * h, HD * (h + 1))
        qh = qc_ref[rows, :]
        for g in range(len(k_refs)):
            s_ref[g, h:h + 1, :] = jnp.sum(k_refs[g][rows, :] * qh, axis=0, keepdims=True)
    for g in range(len(k_refs)):
        lfo_ref[g] = lf_refs[g][...]


def _decode_values_pages(p_ref, pn_ref, vn_ref, v_refs, o_ref, acc_ref, first, last):
    @pl.when(first)
    def _():
        acc_ref[...] = jnp.zeros_like(acc_ref)

    for h in range(H_ATT):
        rows = slice(HD * h, HD * (h + 1))
        acc = acc_ref[rows, :]
        for g in range(len(v_refs)):
            acc = acc + v_refs[g][rows, :] * p_ref[g, h:h + 1, :]
        acc_ref[rows, :] = acc

    @pl.when(last)
    def _():
        pnb = jnp.broadcast_to(pn_ref[...][:, None, :], (H_ATT, HD, PAGE)).reshape(D_ATT, PAGE)
        o_ref[...] = jnp.sum(acc_ref[...], axis=-1, keepdims=True) + pnb * vn_ref[...]


def _dec_softmax_kernel(s_ref, lf_ref, qc_ref, kn_ref, lfn_ref, triu_ref, ones_ref,
                        p_ref, pn_ref, r_ref):
    npg = s_ref.shape[0]
    lf = lf_ref[...].reshape(npg * H_ATT, PAGE)
    r_ref[...] = _dot3_right(lf, triu_ref[...]).reshape(npg, H_ATT, PAGE)
    p_ref[...] = _dot3_right(lf, ones_ref[...]).reshape(npg, H_ATT, PAGE)

    def body(t, carry):
        pg = npg - 1 - t
        r_ref[pg] = r_ref[pg] + carry
        return carry + p_ref[pg]

    lax.fori_loop(0, npg, body, lfn_ref[...])
    s = s_ref[...] + r_ref[...]
    s_new = jnp.sum((qc_ref[...] * kn_ref[...]).reshape(H_ATT, HD, PAGE), axis=1)
    m = jnp.max(jnp.max(s, axis=0), axis=-1, keepdims=True)
    m = jnp.maximum(m, s_new)
    e = jnp.exp(s - m[None])
    e_new = jnp.exp(s_new - m)
    denom = jnp.sum(jnp.sum(e, axis=0), axis=-1, keepdims=True) + e_new
    inv = 1.0 / denom
    p_ref[...] = e * inv[None]
    pn_ref[...] = e_new * inv


def _dec_softmax(s, lfg, qcol, kncol, lfncol, p):
    nb, npg = s.shape[:2]
    blk4 = pl.BlockSpec((None, npg, H_ATT, PAGE), lambda b: (b, 0, 0, 0))
    col = pl.BlockSpec((None, D_ATT, PAGE), lambda b: (b, 0, 0))
    hrow = pl.BlockSpec((None, H_ATT, PAGE), lambda b: (b, 0, 0))
    c128 = pl.BlockSpec((128, 128), lambda b: (0, 0))
    return pl.pallas_call(
        _dec_softmax_kernel,
        out_shape=(jax.ShapeDtypeStruct((nb, npg, H_ATT, PAGE), F32),
                   jax.ShapeDtypeStruct((nb, H_ATT, PAGE), F32)),
        grid=(nb,),
        in_specs=[blk4, blk4, col, col, hrow, c128, c128],
        out_specs=(blk4, hrow),
        scratch_shapes=[pltpu.VMEM((npg, H_ATT, PAGE), F32)],
        compiler_params=_cparams(("arbitrary",)),
        name="decode_softmax",
    )(s, lfg, qcol, kncol, lfncol, p["tri_strict_t"], p["ones128"])


def _block_diag_256(w):
    w4 = w.reshape(4, 4, LRU_BW, LRU_BW)
    eye = jnp.eye(4, dtype=w.dtype)
    bd = jnp.einsum("cjab,jk->cjakb", w4, eye)
    return bd.reshape(4, 256, 256).astype(BF16)


def _pad_lanes(v, width=128):
    return jnp.zeros((1, width), F32).at[0, :v.shape[0]].set(v)


def _constants():
    r = jnp.arange(128)
    tri_incl = (r[None, :] <= r[:, None]).astype(BF16)
    src = jnp.arange(3 * 128)
    dst = jnp.arange(H_ATT * 128)
    sel_big = ((dst[None, :] == 128 * (src[:, None] % 128) + HD + src[:, None] // 128)
               & (src[:, None] % 128 < H_ATT)).astype(BF16)
    aug_const = ((dst % 128 >= HD + 3) & (dst % 128 < HD + 6)).astype(F32)[None]
    return {
        "sel_big": sel_big,
        "aug_const": aug_const,
        "tri_incl": tri_incl,
        "tri_incl_t": tri_incl.T,
        "tri_strict_t": (r[:, None] > r[None, :]).astype(BF16),
        "ones128": jnp.ones((128, 128), BF16),
        "expand": (jnp.arange(D_SSM)[None, :] // P_SSM == r[:, None]).astype(BF16),
        "bd_ones": (jnp.arange(256)[:, None] // HD == jnp.arange(256)[None, :] // HD).astype(BF16),
    }


def _layer_params(l, a, consts):
    p = dict(consts)
    p["lru_conv_w"] = a["lru_conv_w"][l]
    p["lru_conv_b"] = a["lru_conv_b"][l][None]
    p["wa_bd"] = _block_diag_256(a["lru_wa"][l])
    p["wx_bd"] = _block_diag_256(a["lru_wx"][l])
    p["lru_ba"] = a["lru_ba"][l][None]
    p["lru_bx"] = a["lru_bx"][l][None]
    p["lru_lambda"] = a["lru_lambda"][l][None]
    p["ssm_conv_w"] = a["ssm_conv_w"][l]
    p["ssm_conv_b"] = a["ssm_conv_b"][l][None]
    p["dt_bias128"] = _pad_lanes(a["ssm_dt_bias"][l])
    p["a_log128"] = _pad_lanes(a["ssm_a_log"][l])
    p["d_x"] = jnp.repeat(a["ssm_d"][l], P_SSM)[None]
    p["ssm_norm_g"] = a["ssm_norm_g"][l][None]
    p["gq_x"] = jnp.tile(a["att_q_norm_g"][l], H_ATT)[None]
    p["gk_x"] = jnp.tile(a["att_k_norm_g"][l], H_ATT)[None]
    p["f_bias128"] = _pad_lanes(a["att_f_bias"][l])
    p["b_gate"] = a["b_gate"][l]
    p["norm2_g"] = a["norm2_g"][l][None]
    for name in ("w_branch_bf", "w_out_bf", "w_up_bf", "w_down_bf"):
        p[name] = a[name]
    return p


def _in_proj(xn, xsn, w_t, layer, bm):
    lru = _mm_nt(xn, xsn, w_t, layer, OFF_LRU, 2048, 1024, bm, "inproj_lru")
    z = _mm_nt(xn, xsn, w_t, layer, OFF_Z, 1024, 1024, bm, "inproj_z")
    xbc = _mm_nt(xn, xsn, w_t, layer, OFF_XBC, 1536, 768, bm, "inproj_xbc")
    qkv = _mm_nt(xn, xsn, w_t, layer, OFF_QKV, 3072, 1024, bm, "inproj_qkv")
    gates = _mm_nt(xn, xsn, w_t, layer, OFF_GATES, 6144, 1024, bm, "inproj_gates")
    dtf = _mm_small(xn, xsn, w_t, layer, bm)
    return lru, z, xbc, qkv, gates, dtf


def _pad_rows(x, rows):
    nb, w = x.shape
    return jnp.zeros((nb, rows, w), x.dtype).at[:, 0].set(x).reshape(nb * rows, w)


def _tail8(buf):
    return jnp.pad(buf, ((0, 0), (5, 0), (0, 0)))


def kernel(x_prompt, x_sample, cache_k, cache_v, cache_logf, state_lru_h, state_lru_conv, state_ssm, state_ssm_conv, page_table, norm1_g, w_in, b_gate, lru_conv_w, lru_conv_b, lru_wa, lru_ba, lru_wx, lru_bx, lru_lambda, ssm_conv_w, ssm_conv_b, ssm_dt_bias, ssm_a_log, ssm_d, ssm_norm_g, att_q_norm_g, att_k_norm_g, att_f_bias, w_branch, w_out, norm2_g, w_up, w_down):
    a = dict(b_gate=b_gate, lru_conv_w=lru_conv_w, lru_conv_b=lru_conv_b, lru_wa=lru_wa, lru_ba=lru_ba,
             lru_wx=lru_wx, lru_bx=lru_bx, lru_lambda=lru_lambda, ssm_conv_w=ssm_conv_w,
             ssm_conv_b=ssm_conv_b, ssm_dt_bias=ssm_dt_bias, ssm_a_log=ssm_a_log, ssm_d=ssm_d,
             ssm_norm_g=ssm_norm_g, att_q_norm_g=att_q_norm_g, att_k_norm_g=att_k_norm_g,
             att_f_bias=att_f_bias, norm2_g=norm2_g,
             w_branch_bf=w_branch.astype(BF16), w_out_bf=w_out.astype(BF16),
             w_up_bf=w_up.astype(BF16), w_down_bf=w_down.astype(BF16))
    nbp, seq, _ = x_prompt.shape
    nbs = x_sample.shape[0]
    mp = nbp * seq
    npg = seq // PAGE
    n_pool = cache_k.shape[1]
    consts = _constants()

    w_t = jnp.swapaxes(w_in, 1, 2)
    cache_kt = jnp.transpose(cache_k, (0, 1, 3, 4, 2)).reshape(DEPTH, n_pool, D_ATT, PAGE)
    cache_vt = jnp.transpose(cache_v, (0, 1, 3, 4, 2)).reshape(DEPTH, n_pool, D_ATT, PAGE)
    cache_lft = jnp.transpose(cache_logf, (0, 1, 3, 2))

    xp = x_prompt.reshape(mp, D_MODEL)
    xs = x_sample.reshape(nbs, D_MODEL)
    xpn = _rmsnorm(xp, norm1_g[0][None], 512)
    xsn = _rmsnorm(xs, norm1_g[0][None], nbs)

    zeros_h = jnp.zeros((nbp, 1, D_LRU), F32)
    zeros_lt = jnp.zeros((nbp, 8, D_LRU), F32)
    zeros_st = jnp.zeros((nbp, D_SSM, N_SSM), F32)
    zeros_xt = jnp.zeros((nbp, 8, D_XBC), F32)

    outs = [[] for _ in range(14)]
    kv_prev = None
    for l in range(DEPTH):
        p = _layer_params(l, a, consts)
        g_next = norm1_g[(l + 1) % DEPTH][None]

        ((lru, lru_s), (z, z_s), (xbc, xbc_s), (qkv, qkv_s), (gates, gates_s), (dtf, dtf_s)) = _in_proj(
            xpn, xsn, w_t, l, min(1024, mp))

        qs, ks, lfs = _prep_s(qkv_s, dtf_s, p)
        vs = qkv_s[:, 2 * D_ATT:]
        bcast = lambda v: jnp.broadcast_to(v[:, :, None], (nbs, v.shape[1], PAGE))
        qcol = bcast(qs)
        y_lru, h_last = _lru_branch(lru, p, zeros_h, zeros_lt, nbp, seq, 512, 511)
        y_ssm, st_last, s_mat, lfg = _ssd_branch(xbc, z, dtf, p, zeros_st, zeros_xt, nbp, seq, PAGE,
                                                 decode=(page_table, qcol, cache_kt, cache_lft, l))
        pmat, pnew = _dec_softmax(s_mat, lfg, qcol, bcast(ks), bcast(lfs[:, :H_ATT]), p)
        q_aug, kta, vta, kt_all, vt_all, lft_all = _prep(qkv, dtf, p, nbp, seq, l, kv_prev)
        kv_prev = (kt_all, vt_all, lft_all)
        y_att, o_s = _attention(q_aug, kta, vta, nbp, seq,
                                decode=(page_table, pmat, pnew, bcast(vs), cache_vt, l))
        y_att_s = o_s[:, :, 0].astype(BF16)
        outs[6].append(h_last.reshape(nbp, D_LRU))
        outs[8].append(lru.reshape(nbp, seq, 2 * D_LRU)[:, seq - 3:, :D_LRU])
        outs[10].append(st_last.reshape(nbp, H_SSM, P_SSM, N_SSM))
        outs[12].append(xbc.reshape(nbp, seq, D_XBC)[:, seq - 3:, :])

        y_lru_s, h_s = _lru_branch(_pad_rows(lru_s, 8), p, state_lru_h[l][:, None, :],
                                   _tail8(state_lru_conv[l]), nbs, 8, 8, 0)
        y_ssm_s, st_s = _ssd_branch(_pad_rows(xbc_s, PAGE), _pad_rows(z_s, PAGE), _pad_rows(dtf_s, PAGE), p,
                                    state_ssm[l].reshape(nbs, D_SSM, N_SSM), _tail8(state_ssm_conv[l]),
                                    nbs, PAGE, 1)

        x1, x1n, x1s, x1sn = _merge((y_lru, y_ssm, y_att, gates, xp),
                                    (y_lru_s[::8], y_ssm_s[::PAGE], y_att_s, gates_s, xs), p, l, 256)
        xp, xpn, xs, xsn = _mlp((x1n, x1), (x1sn, x1s), p, l, g_next, min(1024, mp))
        outs[3].append(ks.reshape(nbs, 1, H_ATT, HD))
        outs[4].append(vs.reshape(nbs, 1, H_ATT, HD))
        outs[5].append(lfs[:, :H_ATT].reshape(nbs, 1, H_ATT))
        outs[7].append(h_s.reshape(nbs, D_LRU))
        outs[9].append(jnp.concatenate([state_lru_conv[l][:, 1:], lru_s[:, None, :D_LRU]], axis=1))
        outs[11].append(st_s.reshape(nbs, H_SSM, P_SSM, N_SSM))
        outs[13].append(jnp.concatenate([state_ssm_conv[l][:, 1:], xbc_s[:, None, :]], axis=1))

    st = [jnp.stack(o) if o else None for o in outs]
    kt_all, vt_all, lft_all = kv_prev
    new_k_p = jnp.transpose(kt_all.reshape(DEPTH, nbp, npg, H_ATT, HD, PAGE), (0, 1, 2, 5, 3, 4))
    new_v_p = jnp.transpose(vt_all.reshape(DEPTH, nbp, npg, H_ATT, HD, PAGE), (0, 1, 2, 5, 3, 4))
    new_lf_p = jnp.transpose(lft_all, (0, 1, 2, 4, 3))
    return (xp.reshape(nbp, seq, D_MODEL), xs.reshape(nbs, 1, D_MODEL),
            new_k_p, new_v_p, new_lf_p, st[3], st[4], st[5],
            st[6], st[7], st[8], st[9], st[10], st[11], st[12], st[13])
```
